```python
import math
import jax, jax.numpy as jnp
from jax import lax
import numpy as np

D_MODEL = 1024
BATCH = 32
SEQ = 256
DEPTH = 1
DEC_BATCH = 8
DEC_SEQ = 1024
PAST_LEN = 256

GRID_W = 64
N_DIFF_HEADS = 4
DIFF_HEAD_DIM = 64
ATT_WIDTH = N_DIFF_HEADS * 2 * DIFF_HEAD_DIM
N_FOURIER_GROUPS = 4
FOURIER_GROUP_DIM = 128
FOURIER_WIDTH = N_FOURIER_GROUPS * FOURIER_GROUP_DIM
MIX_WIDTH = ATT_WIDTH + FOURIER_WIDTH
IN_WIDTH = 3 * ATT_WIDTH + FOURIER_WIDTH
ROPE_BASE = 10000.0
Q_BLOCK = 128
N_PEER_HEADS = 8
PEER_KEY_DIM = 256
N_KEYS = 128
N_EXPERTS = N_KEYS * N_KEYS
PEER_TOPK = 16
TOKEN_BLOCK = 128
EPS = 1e-6

kernel_name = "hybrid_diffattn_fnet_peer_dit_step"


def rmsnorm(x, g):
    x32 = x.astype(jnp.float32)
    y = x32 * lax.rsqrt(jnp.mean(x32 * x32, axis=-1, keepdims=True) + EPS)
    return y.astype(x.dtype) * g


def modulation(cond, w_mod, b_mod):
    m = jax.nn.silu(cond) @ w_mod + b_mod
    return jnp.split(m[..., None, :], 6, axis=-1)


def axial_rope(n_tok):
    rows = n_tok // GRID_W
    row = jnp.repeat(jnp.arange(rows), GRID_W).astype(jnp.float32)
    col = jnp.tile(jnp.arange(GRID_W), rows).astype(jnp.float32)
    nf = DIFF_HEAD_DIM // 4
    inv = ROPE_BASE ** (-jnp.arange(nf, dtype=jnp.float32) / nf)
    ar = row[:, None] * inv
    ac = col[:, None] * inv
    ang = jnp.concatenate([ar, ar, ac, ac], axis=-1)
    return jnp.cos(ang)[:, None, None, :], jnp.sin(ang)[:, None, None, :]


def apply_rope(x, cos, sin):
    a, b, c, d = jnp.split(x, 4, axis=-1)
    rot = jnp.concatenate([-b, a, -d, c], axis=-1)
    return (x.astype(jnp.float32) * cos + rot.astype(jnp.float32) * sin).astype(x.dtype)


def split_in(h, w_in):
    B, S, _ = h.shape
    z = h @ w_in
    q = z[..., :ATT_WIDTH].reshape(B, S, N_DIFF_HEADS, 2, DIFF_HEAD_DIM)
    k = z[..., ATT_WIDTH:2 * ATT_WIDTH].reshape(B, S, N_DIFF_HEADS, 2, DIFF_HEAD_DIM)
    v = z[..., 2 * ATT_WIDTH:3 * ATT_WIDTH].reshape(B, S, N_DIFF_HEADS, 2 * DIFF_HEAD_DIM)
    f = z[..., 3 * ATT_WIDTH:]
    return q, k, v, f


def diff_lambda(lq1, lk1, lq2, lk2, lam_init):
    return (jnp.exp(jnp.sum((lq1 * lk1).astype(jnp.float32)))
            - jnp.exp(jnp.sum((lq2 * lk2).astype(jnp.float32))) + lam_init)


def diff_attention(q, k, v, lam):
    B, Sq, H, _, d = q.shape
    nb = Sq // Q_BLOCK
    scale = d ** -0.5
    qb = q.reshape(B, nb, Q_BLOCK, H, 2, d).swapaxes(0, 1)

    def block(qblk):
        s = jnp.einsum('bqhmd,bkhmd->mbhqk', qblk, k).astype(jnp.float32) * scale
        p = jax.nn.softmax(s, axis=-1)
        w = p[0] - lam * p[1]
        return jnp.einsum('bhqk,bkhe->bqhe', w.astype(v.dtype), v)

    o = lax.map(block, qb)
    return o.swapaxes(0, 1).reshape(B, Sq, H, 2 * d)


def fourier_mix(f, w_f):
    B, S, _ = f.shape
    fg = f.reshape(B, S, N_FOURIER_GROUPS, FOURIER_GROUP_DIM).astype(jnp.float32)
    spec = jnp.fft.fft2(fg, axes=(1, 3), norm='ortho').real.astype(f.dtype)
    return jnp.einsum('bsgc,gce->bsge', spec, w_f).reshape(B, S, FOURIER_WIDTH)


def mixer_merge(att, four, subln_g, lam_init, w_out):
    B, S = att.shape[:2]
    a = rmsnorm(att, subln_g) * (1.0 - lam_init)
    m = jnp.concatenate([a.reshape(B, S, ATT_WIDTH), four], axis=-1)
    return m @ w_out


def peer(h, w_query, sub_keys, expert_u, expert_v):
    B, S, D = h.shape
    T = B * S
    hf = h.reshape(T, D)
    q = (hf @ w_query).reshape(T, N_PEER_HEADS, 2, PEER_KEY_DIM // 2)
    s = jnp.einsum('thpk,hpnk->thpn', q, sub_keys)
    s1, i1 = lax.top_k(s[:, :, 0], PEER_TOPK)
    s2, i2 = lax.top_k(s[:, :, 1], PEER_TOPK)
    cand = (s1[..., :, None] + s2[..., None, :]).reshape(T, N_PEER_HEADS, PEER_TOPK * PEER_TOPK)
    cand_idx = (i1[..., :, None] * N_KEYS + i2[..., None, :]).reshape(T, N_PEER_HEADS, PEER_TOPK * PEER_TOPK)
    top_s, pos = lax.top_k(cand, PEER_TOPK)
    idx = jnp.take_along_axis(cand_idx, pos, axis=-1)
    g = jax.nn.softmax(top_s.astype(jnp.float32), axis=-1).astype(h.dtype)
    nb = T // TOKEN_BLOCK
    nk = N_PEER_HEADS * PEER_TOPK
    hb = hf.reshape(nb, TOKEN_BLOCK, D)
    ib = idx.reshape(nb, TOKEN_BLOCK, nk)
    gb = g.reshape(nb, TOKEN_BLOCK, nk)

    def block(args):
        hx, ix, gx = args
        a = jax.nn.gelu(jnp.einsum('tkd,td->tk', expert_u[ix], hx), approximate=False)
        return jnp.einsum('tk,tkd->td', gx * a, expert_v[ix])

    out = lax.map(block, (hb, ib, gb))
    return out.reshape(B, S, D)


def setup_inputs(seed: int = 0) -> dict:
    key = jax.random.key(seed)
    ks = jax.random.split(key, 24)
    f32 = jnp.float32
    nrm = lambda k, shp, sc: jax.random.normal(k, shp, f32) * sc
    D = D_MODEL
    cshape = (DEC_BATCH, DEPTH, PAST_LEN, N_DIFF_HEADS, 2 * DIFF_HEAD_DIM)
    return {
        "x_prompt": nrm(ks[0], (BATCH, SEQ, D), 1.0),
        "x_sample": nrm(ks[1], (DEC_BATCH, DEC_SEQ, D), 1.0),
        "cache_k": nrm(ks[2], cshape, 1.0),
        "cache_v": nrm(ks[3], cshape, 1.0),
        "c": nrm(ks[4], (DEC_BATCH, D), 1.0),
        "c_ctx": nrm(ks[5], (D,), 1.0),
        "w_mod": nrm(ks[6], (DEPTH, D, 6 * D), D ** -0.5),
        "b_mod": nrm(ks[7], (DEPTH, 6 * D), 0.01),
        "norm1_g": 1.0 + nrm(ks[8], (DEPTH, D), 0.05),
        "w_in": nrm(ks[9], (DEPTH, D, IN_WIDTH), D ** -0.5),
        "lam_q1": nrm(ks[10], (DEPTH, DIFF_HEAD_DIM), 0.1),
        "lam_k1": nrm(ks[11], (DEPTH, DIFF_HEAD_DIM), 0.1),
        "lam_q2": nrm(ks[12], (DEPTH, DIFF_HEAD_DIM), 0.1),
        "lam_k2": nrm(ks[13], (DEPTH, DIFF_HEAD_DIM), 0.1),
        "subln_g": 1.0 + nrm(ks[14], (DEPTH, 2 * DIFF_HEAD_DIM), 0.05),
        "w_fourier": nrm(ks[15], (DEPTH, N_FOURIER_GROUPS, FOURIER_GROUP_DIM, FOURIER_GROUP_DIM), FOURIER_GROUP_DIM ** -0.5),
        "w_out": nrm(ks[16], (DEPTH, MIX_WIDTH, D), MIX_WIDTH ** -0.5),
        "norm2_g": 1.0 + nrm(ks[17], (DEPTH, D), 0.05),
        "w_query": nrm(ks[18], (DEPTH, D, N_PEER_HEADS * PEER_KEY_DIM), D ** -0.5),
        "sub_keys": nrm(ks[19], (DEPTH, N_PEER_HEADS, 2, N_KEYS, PEER_KEY_DIM // 2), (PEER_KEY_DIM // 2) ** -0.5),
        "expert_u": nrm(ks[20], (DEPTH, N_EXPERTS, D), D ** -0.5),
        "expert_v": nrm(ks[21], (DEPTH, N_EXPERTS, D), 0.5),
        "final_g": 1.0 + nrm(ks[22], (D,), 0.05),
    }


def reference(x_prompt, x_sample, cache_k, cache_v, c, c_ctx, w_mod, b_mod, norm1_g, w_in,
              lam_q1, lam_k1, lam_q2, lam_k2, subln_g, w_fourier, w_out, norm2_g,
              w_query, sub_keys, expert_u, expert_v, final_g):
    B, S, _ = x_prompt.shape
    Bd, Sd, _ = x_sample.shape
    P = cache_k.shape[2]
    cos, sin = axial_rope(Sd)
    xp, xs = x_prompt, x_sample
    new_k, new_v = [], []
    for l in range(DEPTH):
        lam_init = 0.8 - 0.6 * math.exp(-0.3 * l)
        lam = diff_lambda(lam_q1[l], lam_k1[l], lam_q2[l], lam_k2[l], lam_init)

        sh1, sc1, g1, sh2, sc2, g2 = modulation(c_ctx, w_mod[l], b_mod[l])
        h = rmsnorm(xp, norm1_g[l]) * (1 + sc1) + sh1
        q, k, v, f = split_in(h, w_in[l])
        att = diff_attention(q, k, v, lam)
        four = fourier_mix(f, w_fourier[l])
        xp = xp + g1 * mixer_merge(att, four, subln_g[l], lam_init, w_out[l])
        h2 = rmsnorm(xp, norm2_g[l]) * (1 + sc2) + sh2
        xp = xp + g2 * peer(h2, w_query[l], sub_keys[l], expert_u[l], expert_v[l])
        new_k.append(k.reshape(B, S, N_DIFF_HEADS, 2 * DIFF_HEAD_DIM))
        new_v.append(v)

        sh1, sc1, g1, sh2, sc2, g2 = modulation(c, w_mod[l], b_mod[l])
        h = rmsnorm(xs, norm1_g[l]) * (1 + sc1) + sh1
        q, k, v, f = split_in(h, w_in[l])
        q = apply_rope(q, cos, sin)
        k = apply_rope(k, cos, sin)
        ck = cache_k[:, l].reshape(Bd, P, N_DIFF_HEADS, 2, DIFF_HEAD_DIM)
        k_all = jnp.concatenate([ck, k], axis=1)
        v_all = jnp.concatenate([cache_v[:, l], v], axis=1)
        att = diff_attention(q, k_all, v_all, lam)
        four = fourier_mix(f, w_fourier[l])
        xs = xs + g1 * mixer_merge(att, four, subln_g[l], lam_init, w_out[l])
        h2 = rmsnorm(xs, norm2_g[l]) * (1 + sc2) + sh2
        xs = xs + g2 * peer(h2, w_query[l], sub_keys[l], expert_u[l], expert_v[l])

    y_prompt = rmsnorm(xp, final_g)
    y_sample = rmsnorm(xs, final_g)
    state_k = jnp.stack(new_k, axis=1)
    state_v = jnp.stack(new_v, axis=1)
    return (y_prompt, y_sample, state_k, state_v)
```

```python
import functools
import math

import jax
import jax.numpy as jnp
import numpy as np
from jax import lax
from jax.experimental import pallas as pl
from jax.experimental.pallas import tpu as pltpu

F32 = jnp.float32
BF16 = jnp.bfloat16

EPS = 1e-6
LANES = 128
GRID_W = 64
N_HEADS = 4
HEAD_DIM = 64
ATT_W = N_HEADS * 2 * HEAD_DIM
N_GROUPS = 4
GROUP_DIM = 128
ROPE_BASE = 10000.0
PEER_HEADS = 8
N_KEYS = 128
TOPK = 16
VMEM_LIMIT = 56 * 1024 * 1024

_CAND_CNT = [TOPK // (k + 1) for k in range(TOPK)]


def _cparams(sem):
    return pltpu.CompilerParams(dimension_semantics=sem, vmem_limit_bytes=VMEM_LIMIT)


def _mod_kernel(c_ref, w_ref, b_ref, o_ref):
    c = c_ref[...]
    a = c / (1.0 + jnp.exp(-c))
    o_ref[...] = jnp.dot(a, w_ref[...], preferred_element_type=F32,
                         precision=lax.Precision.HIGHEST) + b_ref[...]


def _modulation(cond, w_mod, b_mod):
    rows, d = cond.shape
    n = w_mod.shape[1]
    tn = 1536
    return pl.pallas_call(
        _mod_kernel,
        grid=(n // tn,),
        in_specs=[pl.BlockSpec((rows, d), lambda j: (0, 0)),
                  pl.BlockSpec((d, tn), lambda j: (0, j)),
                  pl.BlockSpec((1, tn), lambda j: (0, j))],
        out_specs=pl.BlockSpec((rows, tn), lambda j: (0, j)),
        out_shape=jax.ShapeDtypeStruct((rows, n), F32),
        compiler_params=_cparams(("arbitrary",)),
        name="modulation",
    )(cond, w_mod, b_mod.reshape(1, n))


def _inproj_kernel(*refs, rope):
    if rope:
        x_ref, mod_ref, g_ref, w_ref, cos_ref, sa_ref, sb_ref, q_ref, k_ref, v_ref, f_ref = refs
    else:
        x_ref, mod_ref, g_ref, w_ref, q_ref, k_ref, v_ref, f_ref = refs
    x = x_ref[...]
    ms = jnp.mean(x * x, axis=-1, keepdims=True)
    xn = x * lax.rsqrt(ms + EPS) * g_ref[...]
    h = xn * (1.0 + mod_ref[0, 1:2, :]) + mod_ref[0, 0:1, :]
    z = jnp.dot(h.astype(BF16), w_ref[...], preferred_element_type=F32)
    scale = HEAD_DIM ** -0.5
    if rope:
        cos, sa, sb = cos_ref[...], sa_ref[...], sb_ref[...]
        for ch in range(ATT_W // LANES):
            lo, hi = ch * LANES, (ch + 1) * LANES
            for base, ref, sc in ((0, q_ref, scale), (ATT_W, k_ref, 1.0)):
                t = z[:, base + lo:base + hi]
                r = t * cos + pltpu.roll(t, LANES - 16, 1) * sa + pltpu.roll(t, 16, 1) * sb
                ref[:, lo:hi] = (r * sc).astype(ref.dtype)
    else:
        q_ref[...] = (z[:, :ATT_W] * scale).astype(q_ref.dtype)
        k_ref[...] = z[:, ATT_W:2 * ATT_W].astype(k_ref.dtype)
    v_ref[...] = z[:, 2 * ATT_W:3 * ATT_W].astype(v_ref.dtype)
    f_ref[...] = z[:, 3 * ATT_W:].astype(f_ref.dtype)


def _inproj(x2d, mod, norm_g, w_in_bf, mod_row, tables, kv_dtype, tm):
    t, d = x2d.shape
    n = w_in_bf.shape[1]
    fw = n - 3 * ATT_W
    rope = tables is not None
    in_specs = [pl.BlockSpec((tm, d), lambda i: (i, 0)),
                pl.BlockSpec((1, 6, d), lambda i: (mod_row(i), 0, 0)),
                pl.BlockSpec((1, d), lambda i: (0, 0)),
                pl.BlockSpec((d, n), lambda i: (0, 0))]
    args = [x2d, mod, norm_g.reshape(1, d), w_in_bf]
    if rope:
        seq = tables[0].shape[0]
        per = seq // tm
        for tb in tables:
            in_specs.append(pl.BlockSpec((tm, LANES), lambda i: (i % per, 0)))
            args.append(tb)
    return pl.pallas_call(
        functools.partial(_inproj_kernel, rope=rope),
        grid=(t // tm,),
        in_specs=in_specs,
        out_specs=[pl.BlockSpec((tm, ATT_W), lambda i: (i, 0)),
                   pl.BlockSpec((tm, ATT_W), lambda i: (i, 0)),
                   pl.BlockSpec((tm, ATT_W), lambda i: (i, 0)),
                   pl.BlockSpec((tm, fw), lambda i: (i, 0))],
        out_shape=[jax.ShapeDtypeStruct((t, ATT_W), BF16),
                   jax.ShapeDtypeStruct((t, ATT_W), kv_dtype),
                   jax.ShapeDtypeStruct((t, ATT_W), kv_dtype),
                   jax.ShapeDtypeStruct((t, fw), BF16)],
        compiler_params=_cparams(("arbitrary",)),
        name="inproj_rope" if rope else "inproj",
    )(*args)


def _rope_tables(seq):
    rows = seq // GRID_W
    row = np.repeat(np.arange(rows), GRID_W).astype(np.float32)
    col = np.tile(np.arange(GRID_W), rows).astype(np.float32)
    nf = HEAD_DIM // 4
    inv = jnp.asarray(ROPE_BASE, F32) ** (-jnp.arange(nf, dtype=F32) / nf)
    ar = jnp.asarray(row)[:, None] * inv
    ac = jnp.asarray(col)[:, None] * inv
    ang = jnp.concatenate([ar, ar, ac, ac], axis=-1)
    cos = jnp.tile(jnp.cos(ang), (1, LANES // HEAD_DIM))
    sin = jnp.tile(jnp.sin(ang), (1, LANES // HEAD_DIM))
    first = (np.arange(LANES) % 32) < 16
    sa = jnp.where(first[None, :], -sin, 0.0)
    sb = jnp.where(first[None, :], 0.0, sin)
    return cos, sa, sb


def _attn_kernel(*refs, n_pieces, lam_init):
    lq1, lk1, lq2, lk2, sg_ref, q_ref = refs[:6]
    kv = refs[6:6 + 2 * n_pieces]
    o_ref = refs[6 + 2 * n_pieces]
    l1 = jnp.sum(lq1[...] * lk1[...], axis=-1, keepdims=True)
    l2 = jnp.sum(lq2[...] * lk2[...], axis=-1, keepdims=True)
    lam = jnp.exp(l1) - jnp.exp(l2) + lam_init
    lane = lax.broadcasted_iota(jnp.int32, (1, LANES), 1)
    nt = (((1,), (1,)), ((), ()))
    for h in range(N_HEADS):
        hs = slice(h * LANES, (h + 1) * LANES)
        qh = q_ref[0, :, hs]
        s0, s1, vs = [], [], []
        for p in range(n_pieces):
            kp = kv[2 * p][0, :, hs].astype(BF16)
            vs.append(kv[2 * p + 1][0, :, hs].astype(BF16))
            k0 = jnp.where(lane < HEAD_DIM, kp, jnp.zeros_like(kp))
            k1 = jnp.where(lane >= HEAD_DIM, kp, jnp.zeros_like(kp))
            s0.append(lax.dot_general(qh, k0, nt, preferred_element_type=F32))
            s1.append(lax.dot_general(qh, k1, nt, preferred_element_type=F32))
        ws = []
        for ss in (s0, s1):
            m = ss[0].max(axis=-1, keepdims=True)
            for s in ss[1:]:
                m = jnp.maximum(m, s.max(axis=-1, keepdims=True))
            es = [jnp.exp(s - m) for s in ss]
            den = es[0].sum(axis=-1, keepdims=True)
            for e in es[1:]:
                den = den + e.sum(axis=-1, keepdims=True)
            inv = 1.0 / den
            ws.append([e * inv for e in es])
        o = None
        for p in range(n_pieces):
            w = (ws[0][p] - lam * ws[1][p]).astype(BF16)
            t = jnp.dot(w, vs[p], preferred_element_type=F32)
            o = t if o is None else o + t
        ms = jnp.mean(o * o, axis=-1, keepdims=True)
        a = o * lax.rsqrt(ms + EPS) * sg_ref[...] * (1.0 - lam_init)
        o_ref[0, :, hs] = a.astype(o_ref.dtype)


def _attention(q, kvs, lams, subln_g, lam_init, tq):
    b, sq, w = q.shape
    n_pieces = len(kvs) // 2
    lam_specs = [pl.BlockSpec((1, HEAD_DIM), lambda i, j: (0, 0)) for _ in range(4)]
    in_specs = lam_specs + [pl.BlockSpec((1, LANES), lambda i, j: (0, 0)),
                            pl.BlockSpec((1, tq, w), lambda i, j: (i, j, 0))]
    for a in kvs:
        in_specs.append(pl.BlockSpec((1, a.shape[1], w), lambda i, j: (i, 0, 0)))
    return pl.pallas_call(
        functools.partial(_attn_kernel, n_pieces=n_pieces, lam_init=lam_init),
        grid=(b, sq // tq),
        in_specs=in_specs,
        out_specs=pl.BlockSpec((1, tq, w), lambda i, j: (i, j, 0)),
        out_shape=jax.ShapeDtypeStruct((b, sq, w), BF16),
        compiler_params=_cparams(("arbitrary", "arbitrary")),
        name="diff_attention_%d" % n_pieces,
    )(*[l.reshape(1, HEAD_DIM) for l in lams], subln_g.reshape(1, LANES), q, *kvs)


def _fourier_kernel(f_ref, wcs_ref, m2_ref, wf_ref, o_ref):
    for g in range(N_GROUPS):
        gs = slice(g * GROUP_DIM, (g + 1) * GROUP_DIM)
        x = f_ref[0, :, gs]
        y = jnp.dot(x, wcs_ref[...], preferred_element_type=F32)
        yc = jnp.concatenate([y[:, :GROUP_DIM], y[:, GROUP_DIM:]], axis=0).astype(BF16)
        spec = jnp.dot(m2_ref[...], yc, preferred_element_type=F32)
        o_ref[0, :, gs] = jnp.dot(spec.astype(BF16), wf_ref[g],
                                  preferred_element_type=F32).astype(o_ref.dtype)


def _dft_tables(seq):
    kc = np.arange(GROUP_DIM)
    ang_c = 2.0 * np.pi * ((kc[:, None] * kc[None, :]) % GROUP_DIM) / GROUP_DIM
    wcs = np.concatenate([np.cos(ang_c), np.sin(ang_c)], axis=1) / math.sqrt(GROUP_DIM)
    ks = np.arange(seq)
    ang_s = 2.0 * np.pi * ((ks[:, None] * ks[None, :]) % seq) / seq
    m2 = np.concatenate([np.cos(ang_s), -np.sin(ang_s)], axis=1) / math.sqrt(seq)
    return jnp.asarray(wcs, F32).astype(BF16), jnp.asarray(m2, F32).astype(BF16)


def _fourier(f, w_f_bf):
    b, s, w = f.shape
    wcs, m2 = _dft_tables(s)
    return pl.pallas_call(
        _fourier_kernel,
        grid=(b,),
        in_specs=[pl.BlockSpec((1, s, w), lambda i: (i, 0, 0)),
                  pl.BlockSpec(wcs.shape, lambda i: (0, 0)),
                  pl.BlockSpec(m2.shape, lambda i: (0, 0)),
                  pl.BlockSpec(w_f_bf.shape, lambda i: (0, 0, 0))],
        out_specs=pl.BlockSpec((1, s, w), lambda i: (i, 0, 0)),
        out_shape=jax.ShapeDtypeStruct((b, s, w), BF16),
        compiler_params=_cparams(("arbitrary",)),
        name="fourier_mix",
    )(f, wcs, m2, w_f_bf)


def _outproj_kernel(x_ref, a_ref, f_ref, wa_ref, wf_ref, mod_ref, o_ref):
    m = jnp.dot(a_ref[...], wa_ref[...], preferred_element_type=F32)
    m = m + jnp.dot(f_ref[...], wf_ref[...], preferred_element_type=F32)
    o_ref[...] = x_ref[...] + mod_ref[0, 2:3, :] * m


def _outproj(x2d, a2d, f2d, w_out_bf, mod, mod_row, tm):
    t, d = x2d.shape
    wa, wf = w_out_bf[:ATT_W], w_out_bf[ATT_W:]
    return pl.pallas_call(
        _outproj_kernel,
        grid=(t // tm,),
        in_specs=[pl.BlockSpec((tm, d), lambda i: (i, 0)),
                  pl.BlockSpec((tm, ATT_W), lambda i: (i, 0)),
                  pl.BlockSpec((tm, wf.shape[0]), lambda i: (i, 0)),
                  pl.BlockSpec(wa.shape, lambda i: (0, 0)),
                  pl.BlockSpec(wf.shape, lambda i: (0, 0)),
                  pl.BlockSpec((1, 6, d), lambda i: (mod_row(i), 0, 0))],
        out_specs=pl.BlockSpec((tm, d), lambda i: (i, 0)),
        out_shape=jax.ShapeDtypeStruct((t, d), F32),
        compiler_params=_cparams(("arbitrary",)),
        name="outproj",
    )(x2d, a2d, f2d, wa, wf, mod)


def _top16(s, ids):
    work = s
    rank = jnp.full(s.shape, float(TOPK), F32)
    vals, sel_ids = [], []
    for r in range(TOPK):
        m = jnp.max(work, axis=0, keepdims=True)
        mi = jnp.min(jnp.where(work == m, ids, 1e9), axis=0, keepdims=True)
        sel = ids == mi
        rank = jnp.where(sel, float(r), rank)
        work = jnp.where(sel, -jnp.inf, work)
        vals.append(m)
        sel_ids.append(mi)
    return rank, vals, sel_ids


def _rows_to_array(rows):
    it = lax.broadcasted_iota(jnp.int32, (TOPK, LANES), 0)
    out = jnp.zeros((TOPK, LANES), F32)
    for r, row in enumerate(rows):
        out = jnp.where(it == r, row, out)
    return out


def _peer_gates(s1, s2):
    ids = lax.broadcasted_iota(jnp.int32, (N_KEYS, LANES), 0).astype(F32)
    r1, v1, _ = _top16(s1, ids)
    r2, v2, _ = _top16(s2, ids)
    v1a, v2a = _rows_to_array(v1), _rows_to_array(v2)
    it8 = lax.broadcasted_iota(jnp.int32, (8, LANES), 0)
    it16 = lax.broadcasted_iota(jnp.int32, (TOPK, LANES), 0)
    blocks = [v1[0] + v2a]
    idb = [it16.astype(F32)]
    for k1 in range(1, 8):
        blocks.append(jnp.where(it8 < _CAND_CNT[k1], v1[k1] + v2a[0:8, :], -jnp.inf))
        idb.append((it8 + k1 * TOPK).astype(F32))
    blocks.append(v1a[8:16, :] + v2[0])
    idb.append(((it8 + 8) * TOPK).astype(F32))
    cand = jnp.concatenate(blocks, axis=0)
    cid = jnp.concatenate(idb, axis=0)
    _, top, pos = _top16(cand, cid)
    z = jnp.zeros((1, LANES), F32)
    cnt = jnp.zeros((TOPK, LANES), F32)
    it16f = it16.astype(F32)
    for r in range(TOPK):
        z = z + jnp.exp(top[r] - top[0])
        cnt = cnt + jnp.where(it16f == jnp.floor(pos[r] * (1.0 / TOPK)), 1.0, 0.0)
    c1 = jnp.zeros((N_KEYS, LANES), F32)
    for k in range(TOPK):
        c1 = jnp.where(r1 == float(k), cnt[k:k + 1, :], c1)
    w1 = jnp.exp(s1 - v1[0]) * (1.0 / z)
    e2 = jnp.exp(s2 - v2[0])
    return r2, e2, c1, w1


def _peer_kernel(x_ref, mod_ref, g2_ref, gf_ref, wq_ref, sk_ref, u_ref, vt_ref, y_ref,
                 h2t_s, s_s, r2_s, e2_s, c1_s, w1_s, at_s, act_s, acc_s, *, tt, ec):
    j = pl.program_id(1)
    nlg = tt // LANES
    nb = ec // N_KEYS

    @pl.when(j == 0)
    def _prologue():
        x = x_ref[...]
        ms = jnp.mean(x * x, axis=-1, keepdims=True)
        xn = x * lax.rsqrt(ms + EPS) * g2_ref[...]
        h2 = xn * (1.0 + mod_ref[0, 4:5, :]) + mod_ref[0, 3:4, :]
        h2t_s[...] = h2.T.astype(BF16)

        def score(blk, carry):
            r0 = pl.multiple_of(blk * N_KEYS, N_KEYS)
            qt = jnp.dot(wq_ref[pl.ds(r0, N_KEYS), :], h2t_s[...], preferred_element_type=F32)
            s_s[blk] = jnp.dot(sk_ref[blk], qt.astype(BF16), preferred_element_type=F32)
            return carry
        lax.fori_loop(0, 2 * PEER_HEADS, score, 0)

        def gates(idx, carry):
            h = idx // nlg
            c0 = pl.multiple_of((idx % nlg) * LANES, LANES)
            r2, e2, c1, w1 = _peer_gates(s_s[2 * h, :, pl.ds(c0, LANES)],
                                         s_s[2 * h + 1, :, pl.ds(c0, LANES)])
            r2_s[h, :, pl.ds(c0, LANES)] = r2
            e2_s[h, :, pl.ds(c0, LANES)] = e2
            c1_s[idx % nlg, pl.ds(h, N_KEYS, stride=PEER_HEADS), :] = c1
            w1_s[idx % nlg, pl.ds(h, N_KEYS, stride=PEER_HEADS), :] = w1
            return carry
        lax.fori_loop(0, PEER_HEADS * nlg, gates, 0)
        acc_s[...] = jnp.zeros_like(acc_s)

    at_s[...] = jnp.dot(u_ref[...], h2t_s[...], preferred_element_type=F32)

    def block(b, carry):
        i1 = j * nb + b
        r0 = pl.multiple_of(b * N_KEYS, N_KEYS)
        for l in range(nlg):
            ls = slice(l * LANES, (l + 1) * LANES)
            g = jnp.zeros((N_KEYS, LANES), F32)
            c8 = c1_s[l, pl.ds(pl.multiple_of(i1 * PEER_HEADS, PEER_HEADS), PEER_HEADS), :]
            w8 = w1_s[l, pl.ds(pl.multiple_of(i1 * PEER_HEADS, PEER_HEADS), PEER_HEADS), :]
            for h in range(PEER_HEADS):
                c, w = c8[h:h + 1, :], w8[h:h + 1, :]
                g = g + jnp.where(r2_s[h, :, ls] < c, e2_s[h, :, ls] * w, 0.0)
            a = at_s[pl.ds(r0, N_KEYS), ls]
            act = g * (0.5 * a * (1.0 + lax.erf(a * (2.0 ** -0.5))))
            act_s[pl.ds(r0, N_KEYS), ls] = act.astype(BF16)
        return carry
    lax.fori_loop(0, nb, block, 0)

    acc_s[...] += jnp.dot(vt_ref[...], act_s[...], preferred_element_type=F32)

    @pl.when(j == pl.num_programs(1) - 1)
    def _epilogue():
        x2 = x_ref[...] + mod_ref[0, 5:6, :] * acc_s[...].T
        ms = jnp.mean(x2 * x2, axis=-1, keepdims=True)
        y_ref[...] = x2 * lax.rsqrt(ms + EPS) * gf_ref[...]


def _peer(x1, mod, mod_row, norm2_g, final_g, wq_t, sk, u_bf, vt_bf, tt, ec):
    t, d = x1.shape
    ne = u_bf.shape[0]
    scratch = [pltpu.VMEM((d, tt), BF16),
               pltpu.VMEM((2 * PEER_HEADS, N_KEYS, tt), F32),
               pltpu.VMEM((PEER_HEADS, N_KEYS, tt), F32),
               pltpu.VMEM((PEER_HEADS, N_KEYS, tt), F32),
               pltpu.VMEM((tt // LANES, N_KEYS * PEER_HEADS, LANES), F32),
               pltpu.VMEM((tt // LANES, N_KEYS * PEER_HEADS, LANES), F32),
               pltpu.VMEM((ec, tt), F32),
               pltpu.VMEM((ec, tt), BF16),
               pltpu.VMEM((d, tt), F32)]
    return pl.pallas_call(
        functools.partial(_peer_kernel, tt=tt, ec=ec),
        grid=(t // tt, ne // ec),
        in_specs=[pl.BlockSpec((tt, d), lambda i, j: (i, 0)),
                  pl.BlockSpec((1, 6, d), lambda i, j: (mod_row(i), 0, 0)),
                  pl.BlockSpec((1, d), lambda i, j: (0, 0)),
                  pl.BlockSpec((1, d), lambda i, j: (0, 0)),
                  pl.BlockSpec(wq_t.shape, lambda i, j: (0, 0)),
                  pl.BlockSpec(sk.shape, lambda i, j: (0, 0, 0)),
                  pl.BlockSpec((ec, d), lambda i, j: (j, 0)),
                  pl.BlockSpec((d, ec), lambda i, j: (0, j))],
        out_specs=pl.BlockSpec((tt, d), lambda i, j: (i, 0)),
        out_shape=jax.ShapeDtypeStruct((t, d), F32),
        scratch_shapes=scratch,
        compiler_params=_cparams(("arbitrary", "arbitrary")),
        name="peer_dense",
    )(x1, mod, norm2_g.reshape(1, d), final_g.reshape(1, d), wq_t, sk, u_bf, vt_bf)


def kernel(x_prompt, x_sample, cache_k, cache_v, c, c_ctx, w_mod, b_mod, norm1_g, w_in, lam_q1, lam_k1,
           lam_q2, lam_k2, subln_g, w_fourier, w_out, norm2_g, w_query, sub_keys, expert_u, expert_v,
           final_g):
    b, s, d = x_prompt.shape
    bd, sd, _ = x_sample.shape
    p = cache_k.shape[2]
    depth = w_mod.shape[0]
    assert depth == 1
    tm = 512
    tt, ec = 512, 1024

    rows = 16
    cond = jnp.zeros((rows, d), F32).at[0].set(c_ctx).at[1:1 + bd].set(c)
    tables = _rope_tables(sd)

    xp = x_prompt.reshape(b * s, d)
    xs = x_sample.reshape(bd * sd, d)
    l = 0
    lam_init = 0.8 - 0.6 * math.exp(-0.3 * l)
    mod = _modulation(cond, w_mod[l], b_mod[l]).reshape(rows, 6, d)
    w_in_bf = w_in[l].astype(BF16)
    w_out_bf = w_out[l].astype(BF16)
    w_f_bf = w_fourier[l].astype(BF16)
    wq_t = w_query[l].T.astype(BF16)
    sk = sub_keys[l].reshape(2 * PEER_HEADS, N_KEYS, -1).astype(BF16)
    u_bf = expert_u[l].astype(BF16)
    vt_bf = expert_v[l].T.astype(BF16)
    lams = (lam_q1[l], lam_k1[l], lam_q2[l], lam_k2[l])

    ctx_row = lambda i: 0
    lat_row = lambda i: 1 + i // (sd // tm)

    q, k, v, f = _inproj(xp, mod, norm1_g[l], w_in_bf, ctx_row, None, F32, tm)
    k3, v3 = k.reshape(b, s, ATT_W), v.reshape(b, s, ATT_W)
    att = _attention(q.reshape(b, s, ATT_W), (k3, v3), lams, subln_g[l], lam_init, s)
    four = _fourier(f.reshape(b, s, -1), w_f_bf)
    x1 = _outproj(xp, att.reshape(b * s, ATT_W), four.reshape(b * s, -1), w_out_bf, mod, ctx_row, tm)
    y_prompt = _peer(x1, mod, ctx_row, norm2_g[l], final_g, wq_t, sk, u_bf, vt_bf, tt, ec)
    state_k = k.reshape(b, 1, s, N_HEADS, 2 * HEAD_DIM)
    state_v = v.reshape(b, 1, s, N_HEADS, 2 * HEAD_DIM)

    q, k, v, f = _inproj(xs, mod, norm1_g[l], w_in_bf, lat_row, tables, BF16, tm)
    ck = cache_k[:, l].reshape(bd, p, ATT_W)
    cv = cache_v[:, l].reshape(bd, p, ATT_W)
    att = _attention(q.reshape(bd, sd, ATT_W),
                     (ck, cv, k.reshape(bd, sd, ATT_W), v.reshape(bd, sd, ATT_W)),
                     lams, subln_g[l], lam_init, 256)
    four = _fourier(f.reshape(bd, sd, -1), w_f_bf)
    x1 = _outproj(xs, att.reshape(bd * sd, ATT_W), four.reshape(bd * sd, -1), w_out_bf, mod, lat_row, tm)
    y_sample = _peer(x1, mod, lat_row, norm2_g[l], final_g, wq_t, sk, u_bf, vt_bf, tt, ec)

    return (y_prompt.reshape(b, s, d), y_sample.reshape(bd, sd, d), state_k, state_v)
```

```python
import functools
import math

import jax
import jax.numpy as jnp
import numpy as np
from jax import lax
from jax.experimental import pallas as pl
from jax.experimental.pallas import tpu as pltpu

F32 = jnp.float32
BF16 = jnp.bfloat16

EPS = 1e-6
LANES = 128
GRID_W = 64
N_HEADS = 4
HEAD_DIM = 64
ATT_W = N_HEADS * 2 * HEAD_DIM
N_GROUPS = 4
GROUP_DIM = 128
ROPE_BASE = 10000.0
PEER_HEADS = 8
N_KEYS = 128
TOPK = 16
VMEM_LIMIT = 56 * 1024 * 1024

_CAND_CNT = [TOPK // (k + 1) for k in range(TOPK)]


def _cparams(sem):
    return pltpu.CompilerParams(dimension_semantics=sem, vmem_limit_bytes=VMEM_LIMIT)


def _mod_kernel(c_ref, w_ref, b_ref, o_ref):
    c = c_ref[...]
    a = c / (1.0 + jnp.exp(-c))
    o_ref[...] = jnp.dot(a, w_ref[...], preferred_element_type=F32,
                         precision=lax.Precision.HIGHEST) + b_ref[...]


def _modulation(cond, w_mod, b_mod):
    rows, d = cond.shape
    n = w_mod.shape[1]
    tn = 1536
    return pl.pallas_call(
        _mod_kernel,
        grid=(n // tn,),
        in_specs=[pl.BlockSpec((rows, d), lambda j: (0, 0)),
                  pl.BlockSpec((d, tn), lambda j: (0, j)),
                  pl.BlockSpec((1, tn), lambda j: (0, j))],
        out_specs=pl.BlockSpec((rows, tn), lambda j: (0, j)),
        out_shape=jax.ShapeDtypeStruct((rows, n), F32),
        compiler_params=_cparams(("arbitrary",)),
        name="modulation",
    )(cond, w_mod, b_mod.reshape(1, n))


def _inproj_kernel(*refs, rope):
    if rope:
        x_ref, mod_ref, g_ref, w_ref, cos_ref, sa_ref, sb_ref, q_ref, k_ref, v_ref, f_ref = refs
    else:
        x_ref, mod_ref, g_ref, w_ref, q_ref, k_ref, v_ref, f_ref = refs
    x = x_ref[...]
    ms = jnp.mean(x * x, axis=-1, keepdims=True)
    xn = x * lax.rsqrt(ms + EPS) * g_ref[...]
    h = xn * (1.0 + mod_ref[0, 1:2, :]) + mod_ref[0, 0:1, :]
    z = jnp.dot(h.astype(BF16), w_ref[...], preferred_element_type=F32)
    scale = HEAD_DIM ** -0.5
    if rope:
        cos, sa, sb = cos_ref[...], sa_ref[...], sb_ref[...]
        for ch in range(ATT_W // LANES):
            lo, hi = ch * LANES, (ch + 1) * LANES
            for base, ref, sc in ((0, q_ref, scale), (ATT_W, k_ref, 1.0)):
                t = z[:, base + lo:base + hi]
                r = t * cos + pltpu.roll(t, LANES - 16, 1) * sa + pltpu.roll(t, 16, 1) * sb
                ref[:, lo:hi] = (r * sc).astype(ref.dtype)
    else:
        q_ref[...] = (z[:, :ATT_W] * scale).astype(q_ref.dtype)
        k_ref[...] = z[:, ATT_W:2 * ATT_W].astype(k_ref.dtype)
    v_ref[...] = z[:, 2 * ATT_W:3 * ATT_W].astype(v_ref.dtype)
    f_ref[...] = z[:, 3 * ATT_W:].astype(f_ref.dtype)


def _inproj(x2d, mod, norm_g, w_in_bf, mod_row, tables, kv_dtype, tm):
    t, d = x2d.shape
    n = w_in_bf.shape[1]
    fw = n - 3 * ATT_W
    rope = tables is not None
    in_specs = [pl.BlockSpec((tm, d), lambda i: (i, 0)),
                pl.BlockSpec((1, 6, d), lambda i: (mod_row(i), 0, 0)),
                pl.BlockSpec((1, d), lambda i: (0, 0)),
                pl.BlockSpec((d, n), lambda i: (0, 0))]
    args = [x2d, mod, norm_g.reshape(1, d), w_in_bf]
    if rope:
        seq = tables[0].shape[0]
        per = seq // tm
        for tb in tables:
            in_specs.append(pl.BlockSpec((tm, LANES), lambda i: (i % per, 0)))
            args.append(tb)
    return pl.pallas_call(
        functools.partial(_inproj_kernel, rope=rope),
        grid=(t // tm,),
        in_specs=in_specs,
        out_specs=[pl.BlockSpec((tm, ATT_W), lambda i: (i, 0)),
                   pl.BlockSpec((tm, ATT_W), lambda i: (i, 0)),
                   pl.BlockSpec((tm, ATT_W), lambda i: (i, 0)),
                   pl.BlockSpec((tm, fw), lambda i: (i, 0))],
        out_shape=[jax.ShapeDtypeStruct((t, ATT_W), BF16),
                   jax.ShapeDtypeStruct((t, ATT_W), kv_dtype),
                   jax.ShapeDtypeStruct((t, ATT_W), kv_dtype),
                   jax.ShapeDtypeStruct((t, fw), BF16)],
        compiler_params=_cparams(("arbitrary",)),
        name="inproj_rope" if rope else "inproj",
    )(*args)


def _rope_tables(seq):
    rows = seq // GRID_W
    row = np.repeat(np.arange(rows), GRID_W).astype(np.float32)
    col = np.tile(np.arange(GRID_W), rows).astype(np.float32)
    nf = HEAD_DIM // 4
    inv = jnp.asarray(ROPE_BASE, F32) ** (-jnp.arange(nf, dtype=F32) / nf)
    ar = jnp.asarray(row)[:, None] * inv
    ac = jnp.asarray(col)[:, None] * inv
    ang = jnp.concatenate([ar, ar, ac, ac], axis=-1)
    cos = jnp.tile(jnp.cos(ang), (1, LANES // HEAD_DIM))
    sin = jnp.tile(jnp.sin(ang), (1, LANES // HEAD_DIM))
    first = (np.arange(LANES) % 32) < 16
    sa = jnp.where(first[None, :], -sin, 0.0)
    sb = jnp.where(first[None, :], 0.0, sin)
    return cos, sa, sb


def _attn_kernel(*refs, n_pieces, lam_init):
    lq1, lk1, lq2, lk2, sg_ref, q_ref = refs[:6]
    kv = refs[6:6 + 2 * n_pieces]
    o_ref = refs[6 + 2 * n_pieces]
    l1 = jnp.sum(lq1[...] * lk1[...], axis=-1, keepdims=True)
    l2 = jnp.sum(lq2[...] * lk2[...], axis=-1, keepdims=True)
    lam = jnp.exp(l1) - jnp.exp(l2) + lam_init
    lane = lax.broadcasted_iota(jnp.int32, (1, LANES), 1)
    nt = (((1,), (1,)), ((), ()))
    for h in range(N_HEADS):
        hs = slice(h * LANES, (h + 1) * LANES)
        qh = q_ref[0, :, hs]
        s0, s1, vs = [], [], []
        for p in range(n_pieces):
            kp = kv[2 * p][0, :, hs].astype(BF16)
            vs.append(kv[2 * p + 1][0, :, hs].astype(BF16))
            k0 = jnp.where(lane < HEAD_DIM, kp, jnp.zeros_like(kp))
            k1 = jnp.where(lane >= HEAD_DIM, kp, jnp.zeros_like(kp))
            s0.append(lax.dot_general(qh, k0, nt, preferred_element_type=F32))
            s1.append(lax.dot_general(qh, k1, nt, preferred_element_type=F32))
        ws = []
        for ss in (s0, s1):
            m = ss[0].max(axis=-1, keepdims=True)
            for s in ss[1:]:
                m = jnp.maximum(m, s.max(axis=-1, keepdims=True))
            es = [jnp.exp(s - m) for s in ss]
            den = es[0].sum(axis=-1, keepdims=True)
            for e in es[1:]:
                den = den + e.sum(axis=-1, keepdims=True)
            inv = 1.0 / den
            ws.append([e * inv for e in es])
        o = None
        for p in range(n_pieces):
            w = (ws[0][p] - lam * ws[1][p]).astype(BF16)
            t = jnp.dot(w, vs[p], preferred_element_type=F32)
            o = t if o is None else o + t
        ms = jnp.mean(o * o, axis=-1, keepdims=True)
        a = o * lax.rsqrt(ms + EPS) * sg_ref[...] * (1.0 - lam_init)
        o_ref[0, :, hs] = a.astype(o_ref.dtype)


def _attention(q, kvs, lams, subln_g, lam_init, tq):
    b, sq, w = q.shape
    n_pieces = len(kvs) // 2
    lam_specs = [pl.BlockSpec((1, HEAD_DIM), lambda i, j: (0, 0)) for _ in range(4)]
    in_specs = lam_specs + [pl.BlockSpec((1, LANES), lambda i, j: (0, 0)),
                            pl.BlockSpec((1, tq, w), lambda i, j: (i, j, 0))]
    for a in kvs:
        in_specs.append(pl.BlockSpec((1, a.shape[1], w), lambda i, j: (i, 0, 0)))
    return pl.pallas_call(
        functools.partial(_attn_kernel, n_pieces=n_pieces, lam_init=lam_init),
        grid=(b, sq // tq),
        in_specs=in_specs,
        out_specs=pl.BlockSpec((1, tq, w), lambda i, j: (i, j, 0)),
        out_shape=jax.ShapeDtypeStruct((b, sq, w), BF16),
        compiler_params=_cparams(("arbitrary", "arbitrary")),
        name="diff_attention_%d" % n_pieces,
    )(*[l.reshape(1, HEAD_DIM) for l in lams], subln_g.reshape(1, LANES), q, *kvs)


def _fourier_kernel(f_ref, wcs_ref, m2_ref, wf_ref, o_ref):
    for g in range(N_GROUPS):
        gs = slice(g * GROUP_DIM, (g + 1) * GROUP_DIM)
        x = f_ref[0, :, gs]
        y = jnp.dot(x, wcs_ref[...], preferred_element_type=F32)
        yc = jnp.concatenate([y[:, :GROUP_DIM], y[:, GROUP_DIM:]], axis=0).astype(BF16)
        spec = jnp.dot(m2_ref[...], yc, preferred_element_type=F32)
        o_ref[0, :, gs] = jnp.dot(spec.astype(BF16), wf_ref[g],
                                  preferred_element_type=F32).astype(o_ref.dtype)


def _dft_tables(seq):
    kc = np.arange(GROUP_DIM)
    ang_c = 2.0 * np.pi * ((kc[:, None] * kc[None, :]) % GROUP_DIM) / GROUP_DIM
    wcs = np.concatenate([np.cos(ang_c), np.sin(ang_c)], axis=1) / math.sqrt(GROUP_DIM)
    ks = np.arange(seq)
    ang_s = 2.0 * np.pi * ((ks[:, None] * ks[None, :]) % seq) / seq
    m2 = np.concatenate([np.cos(ang_s), -np.sin(ang_s)], axis=1) / math.sqrt(seq)
    return jnp.asarray(wcs, F32).astype(BF16), jnp.asarray(m2, F32).astype(BF16)


def _fourier(f, w_f_bf):
    b, s, w = f.shape
    wcs, m2 = _dft_tables(s)
    return pl.pallas_call(
        _fourier_kernel,
        grid=(b,),
        in_specs=[pl.BlockSpec((1, s, w), lambda i: (i, 0, 0)),
                  pl.BlockSpec(wcs.shape, lambda i: (0, 0)),
                  pl.BlockSpec(m2.shape, lambda i: (0, 0)),
                  pl.BlockSpec(w_f_bf.shape, lambda i: (0, 0, 0))],
        out_specs=pl.BlockSpec((1, s, w), lambda i: (i, 0, 0)),
        out_shape=jax.ShapeDtypeStruct((b, s, w), BF16),
        compiler_params=_cparams(("arbitrary",)),
        name="fourier_mix",
    )(f, wcs, m2, w_f_bf)


def _outproj_kernel(x_ref, a_ref, f_ref, wa_ref, wf_ref, mod_ref, o_ref):
    m = jnp.dot(a_ref[...], wa_ref[...], preferred_element_type=F32)
    m = m + jnp.dot(f_ref[...], wf_ref[...], preferred_element_type=F32)
    o_ref[...] = x_ref[...] + mod_ref[0, 2:3, :] * m


def _outproj(x2d, a2d, f2d, w_out_bf, mod, mod_row, tm):
    t, d = x2d.shape
    wa, wf = w_out_bf[:ATT_W], w_out_bf[ATT_W:]
    return pl.pallas_call(
        _outproj_kernel,
        grid=(t // tm,),
        in_specs=[pl.BlockSpec((tm, d), lambda i: (i, 0)),
                  pl.BlockSpec((tm, ATT_W), lambda i: (i, 0)),
                  pl.BlockSpec((tm, wf.shape[0]), lambda i: (i, 0)),
                  pl.BlockSpec(wa.shape, lambda i: (0, 0)),
                  pl.BlockSpec(wf.shape, lambda i: (0, 0)),
                  pl.BlockSpec((1, 6, d), lambda i: (mod_row(i), 0, 0))],
        out_specs=pl.BlockSpec((tm, d), lambda i: (i, 0)),
        out_shape=jax.ShapeDtypeStruct((t, d), F32),
        compiler_params=_cparams(("arbitrary",)),
        name="outproj",
    )(x2d, a2d, f2d, wa, wf, mod)


def _top16(s, ids, exact, want_rank=True):
    work = s
    rank = jnp.full(s.shape, float(TOPK), F32) if want_rank else None
    vals = []
    for r in range(TOPK):
        m = jnp.max(work, axis=0, keepdims=True)
        sel = work == m
        if exact:
            sel = ids == jnp.min(jnp.where(sel, ids, 1e9), axis=0, keepdims=True)
        if want_rank:
            rank = jnp.where(sel, float(r), rank)
        work = jnp.where(sel, -jnp.inf, work)
        vals.append(m)
    return rank, vals


def _rows_to_array(rows):
    it = lax.broadcasted_iota(jnp.int32, (TOPK, LANES), 0)
    out = jnp.zeros((TOPK, LANES), F32)
    for r, row in enumerate(rows):
        out = jnp.where(it == r, row, out)
    return out


def _candidates(v1a, v2a, exact):
    it8 = lax.broadcasted_iota(jnp.int32, (8, LANES), 0)
    it16 = lax.broadcasted_iota(jnp.int32, (TOPK, LANES), 0)
    blocks = [v1a[0:1, :] + v2a]
    idb = [it16.astype(F32)]
    for k1 in range(1, 8):
        blocks.append(jnp.where(it8 < _CAND_CNT[k1], v1a[k1:k1 + 1, :] + v2a[0:8, :], -jnp.inf))
        idb.append((it8 + k1 * TOPK).astype(F32))
    blocks.append(v1a[8:16, :] + v2a[0:1, :])
    idb.append(((it8 + 8) * TOPK).astype(F32))
    cand = jnp.concatenate(blocks, axis=0)
    cid = jnp.concatenate(idb, axis=0)
    rc, top = _top16(cand, cid, exact)
    z = jnp.zeros((1, LANES), F32)
    for r in range(TOPK):
        z = z + jnp.exp(top[r] - top[0])
    picked = jnp.where(rc < float(TOPK), 1.0, 0.0)
    cnt_lo = jnp.zeros((8, LANES), F32)
    for k1 in range(8):
        lo = 0 if k1 == 0 else 8 + 8 * k1
        n = jnp.sum(picked[lo:lo + (TOPK if k1 == 0 else 8), :], axis=0, keepdims=True)
        cnt_lo = jnp.where(it8 == k1, n, cnt_lo)
    cnt = jnp.concatenate([cnt_lo, picked[72:80, :]], axis=0)
    return cnt, z


def _count_ne16(mask_f32):
    return jnp.where(jnp.sum(mask_f32, axis=0, keepdims=True) != float(TOPK), 1.0, 0.0)


def _peer_gates_exact(s1, s2):
    ids = lax.broadcasted_iota(jnp.int32, (N_KEYS, LANES), 0).astype(F32)
    r1, v1 = _top16(s1, ids, True)
    r2, v2 = _top16(s2, ids, True)
    cnt, z = _candidates(_rows_to_array(v1), _rows_to_array(v2), True)
    c1 = jnp.zeros((N_KEYS, LANES), F32)
    for k in range(TOPK):
        c1 = jnp.where(r1 == float(k), cnt[k:k + 1, :], c1)
    w1 = jnp.exp(s1 - v1[0]) * (1.0 / z)
    e2 = jnp.exp(s2 - v2[0])
    return r2, e2, c1, w1


def _peer_kernel(x_ref, mod_ref, g2_ref, gf_ref, wq_ref, sk_ref, u_ref, vt_ref, y_ref,
                 h2t_s, qt_s, s_s, v_s, r2_s, e2_s, c1_s, w1_s, at_s, act_s, acc_s, *, tt, ec):
    j = pl.program_id(1)
    nlg = tt // LANES
    nb = ec // N_KEYS

    @pl.when(j == 0)
    def _prologue():
        x = x_ref[...]
        ms = jnp.mean(x * x, axis=-1, keepdims=True)
        xn = x * lax.rsqrt(ms + EPS) * g2_ref[...]
        h2 = xn * (1.0 + mod_ref[0, 4:5, :]) + mod_ref[0, 3:4, :]
        h2t_s[...] = h2.T.astype(BF16)

        qt_s[...] = jnp.dot(wq_ref[...], h2t_s[...], preferred_element_type=F32).astype(BF16)
        for blk in range(2 * PEER_HEADS):
            s_s[blk] = jnp.dot(sk_ref[blk], qt_s[blk * N_KEYS:(blk + 1) * N_KEYS, :],
                               preferred_element_type=F32)

        n_items = PEER_HEADS * nlg
        want = float(TOPK)

        def scores(idx):
            h, lg = idx // nlg, idx % nlg
            c0 = pl.multiple_of(lg * LANES, LANES)
            return h, lg, s_s[2 * h, :, pl.ds(c0, LANES)], s_s[2 * h + 1, :, pl.ds(c0, LANES)]

        def store_i1(h, lg, c1, w1):
            c1_s[lg, pl.ds(h, N_KEYS, stride=PEER_HEADS), :] = c1
            w1_s[lg, pl.ds(h, N_KEYS, stride=PEER_HEADS), :] = w1

        def sides(idx, bad):
            h, lg, s1, s2 = scores(idx)
            _, v1 = _top16(s1, None, False, want_rank=False)
            r2, v2 = _top16(s2, None, False)
            r2_s[h, lg] = r2.astype(BF16)
            e2_s[h, lg] = jnp.exp(s2 - v2[0]).astype(BF16)
            v_s[idx, 0:TOPK, :] = _rows_to_array(v1)
            v_s[idx, TOPK:2 * TOPK, :] = _rows_to_array(v2)
            return (bad + _count_ne16(jnp.where(s1 >= v1[TOPK - 1], 1.0, 0.0))
                    + _count_ne16(jnp.where(r2 < want, 1.0, 0.0)))
        bad = lax.fori_loop(0, n_items, sides, jnp.zeros((1, LANES), F32))

        def cells(pair, bad):
            for t in range(2):
                idx = 2 * pair + t
                h, lg, s1, _ = scores(idx)
                v1a = v_s[idx, 0:TOPK, :]
                cnt, z = _candidates(v1a, v_s[idx, TOPK:2 * TOPK, :], False)
                c1 = jnp.zeros((N_KEYS, LANES), F32)
                for k in range(TOPK):
                    c1 = jnp.where(s1 == v1a[k:k + 1, :], cnt[k:k + 1, :], c1)
                store_i1(h, lg, c1, jnp.exp(s1 - v1a[0:1, :]) * (1.0 / z))
                bad = bad + _count_ne16(cnt)
            return bad
        bad = lax.fori_loop(0, n_items // 2, cells, bad)

        @pl.when(jnp.max(bad) > 0.0)
        def _ties():
            def exact_item(idx, carry):
                h, lg, s1, s2 = scores(idx)
                r2, e2, c1, w1 = _peer_gates_exact(s1, s2)
                r2_s[h, lg] = r2.astype(BF16)
                e2_s[h, lg] = e2.astype(BF16)
                store_i1(h, lg, c1, w1)
                return carry
            lax.fori_loop(0, n_items, exact_item, 0)

        acc_s[...] = jnp.zeros_like(acc_s)

    def gate_blocks(b_lo, b_hi):
        for b in range(b_lo, b_hi):
            i1 = pl.multiple_of((j * nb + b) * PEER_HEADS, PEER_HEADS)
            rs = slice(b * N_KEYS, (b + 1) * N_KEYS)
            for l in range(nlg):
                ls = slice(l * LANES, (l + 1) * LANES)
                c8 = c1_s[l, pl.ds(i1, PEER_HEADS), :].astype(BF16)
                w8 = w1_s[l, pl.ds(i1, PEER_HEADS), :].astype(BF16)
                g = jnp.zeros((N_KEYS, LANES), BF16)
                for h in range(PEER_HEADS):
                    g = g + jnp.where(r2_s[h, l] < c8[h:h + 1, :], e2_s[h, l] * w8[h:h + 1, :],
                                      jnp.zeros((), BF16))
                a = at_s[rs, ls]
                gelu = 0.5 * a * (1.0 + lax.erf(a * (2.0 ** -0.5)))
                act_s[rs, ls] = g * gelu.astype(BF16)

    half = ec // 2
    h2t = h2t_s[...]
    at_s[0:half, 0:tt] = jnp.dot(u_ref[0:half, :], h2t, preferred_element_type=F32)
    gate_blocks(0, nb // 2)
    at_s[half:ec, 0:tt] = jnp.dot(u_ref[half:ec, :], h2t, preferred_element_type=F32)
    acc_s[:, 0:tt] += jnp.dot(vt_ref[:, 0:half], act_s[0:half, 0:tt], preferred_element_type=F32)
    gate_blocks(nb // 2, nb)
    acc_s[:, 0:tt] += jnp.dot(vt_ref[:, half:ec], act_s[half:ec, 0:tt], preferred_element_type=F32)

    @pl.when(j == pl.num_programs(1) - 1)
    def _epilogue():
        x2 = x_ref[...] + mod_ref[0, 5:6, :] * acc_s[:, 0:tt].T
        ms = jnp.mean(x2 * x2, axis=-1, keepdims=True)
        y_ref[...] = x2 * lax.rsqrt(ms + EPS) * gf_ref[...]


def _peer(x1, mod, mod_row, norm2_g, final_g, wq_t, sk, u_bf, vt_bf, tt, ec):
    t, d = x1.shape
    ne = u_bf.shape[0]
    scratch = [pltpu.VMEM((d, tt), BF16),
               pltpu.VMEM((wq_t.shape[0], tt), BF16),
               pltpu.VMEM((2 * PEER_HEADS, N_KEYS, tt), F32),
               pltpu.VMEM((PEER_HEADS * tt // LANES, 2 * TOPK, LANES), F32),
               pltpu.VMEM((PEER_HEADS, tt // LANES, N_KEYS, LANES), BF16),
               pltpu.VMEM((PEER_HEADS, tt // LANES, N_KEYS, LANES), BF16),
               pltpu.VMEM((tt // LANES, N_KEYS * PEER_HEADS, LANES), F32),
               pltpu.VMEM((tt // LANES, N_KEYS * PEER_HEADS, LANES), F32),
               pltpu.VMEM((ec, tt + LANES), F32),
               pltpu.VMEM((ec, tt + LANES), BF16),
               pltpu.VMEM((d, tt + LANES), F32)]
    return pl.pallas_call(
        functools.partial(_peer_kernel, tt=tt, ec=ec),
        grid=(t // tt, ne // ec),
        in_specs=[pl.BlockSpec((tt, d), lambda i, j: (i, 0)),
                  pl.BlockSpec((1, 6, d), lambda i, j: (mod_row(i), 0, 0)),
                  pl.BlockSpec((1, d), lambda i, j: (0, 0)),
                  pl.BlockSpec((1, d), lambda i, j: (0, 0)),
                  pl.BlockSpec(wq_t.shape, lambda i, j: (0, 0)),
                  pl.BlockSpec(sk.shape, lambda i, j: (0, 0, 0)),
                  pl.BlockSpec((ec, d), lambda i, j: (j, 0)),
                  pl.BlockSpec((d, ec), lambda i, j: (0, j))],
        out_specs=pl.BlockSpec((tt, d), lambda i, j: (i, 0)),
        out_shape=jax.ShapeDtypeStruct((t, d), F32),
        scratch_shapes=scratch,
        compiler_params=_cparams(("arbitrary", "arbitrary")),
        name="peer_dense",
    )(x1, mod, norm2_g.reshape(1, d), final_g.reshape(1, d), wq_t, sk, u_bf, vt_bf)


def kernel(x_prompt, x_sample, cache_k, cache_v, c, c_ctx, w_mod, b_mod, norm1_g, w_in, lam_q1, lam_k1,
           lam_q2, lam_k2, subln_g, w_fourier, w_out, norm2_g, w_query, sub_keys, expert_u, expert_v,
           final_g):
    b, s, d = x_prompt.shape
    bd, sd, _ = x_sample.shape
    p = cache_k.shape[2]
    depth = w_mod.shape[0]
    assert depth == 1
    tm = 512
    tt, ec = 512, 1024

    rows = 16
    cond = jnp.zeros((rows, d), F32).at[0].set(c_ctx).at[1:1 + bd].set(c)
    tables = _rope_tables(sd)

    xp = x_prompt.reshape(b * s, d)
    xs = x_sample.reshape(bd * sd, d)
    l = 0
    lam_init = 0.8 - 0.6 * math.exp(-0.3 * l)
    mod = _modulation(cond, w_mod[l], b_mod[l]).reshape(rows, 6, d)
    w_in_bf = w_in[l].astype(BF16)
    w_out_bf = w_out[l].astype(BF16)
    w_f_bf = w_fourier[l].astype(BF16)
    wq_t = w_query[l].T.astype(BF16)
    sk = sub_keys[l].reshape(2 * PEER_HEADS, N_KEYS, -1).astype(BF16)
    u_bf = expert_u[l].astype(BF16)
    vt_bf = expert_v[l].T.astype(BF16)
    lams = (lam_q1[l], lam_k1[l], lam_q2[l], lam_k2[l])

    ctx_row = lambda i: 0
    lat_row = lambda i: 1 + i // (sd // tm)

    q, k, v, f = _inproj(xp, mod, norm1_g[l], w_in_bf, ctx_row, None, F32, tm)
    k3, v3 = k.reshape(b, s, ATT_W), v.reshape(b, s, ATT_W)
    att = _attention(q.reshape(b, s, ATT_W), (k3, v3), lams, subln_g[l], lam_init, s)
    four = _fourier(f.reshape(b, s, -1), w_f_bf)
    x1 = _outproj(xp, att.reshape(b * s, ATT_W), four.reshape(b * s, -1), w_out_bf, mod, ctx_row, tm)
    y_prompt = _peer(x1, mod, ctx_row, norm2_g[l], final_g, wq_t, sk, u_bf, vt_bf, tt, ec)
    state_k = k.reshape(b, 1, s, N_HEADS, 2 * HEAD_DIM)
    state_v = v.reshape(b, 1, s, N_HEADS, 2 * HEAD_DIM)

    q, k, v, f = _inproj(xs, mod, norm1_g[l], w_in_bf, lat_row, tables, BF16, tm)
    ck = cache_k[:, l].reshape(bd, p, ATT_W)
    cv = cache_v[:, l].reshape(bd, p, ATT_W)
    att = _attention(q.reshape(bd, sd, ATT_W),
                     (ck, cv, k.reshape(bd, sd, ATT_W), v.reshape(bd, sd, ATT_W)),
                     lams, subln_g[l], lam_init, 256)
    four = _fourier(f.reshape(bd, sd, -1), w_f_bf)
    x1 = _outproj(xs, att.reshape(bd * sd, ATT_W), four.reshape(bd * sd, -1), w_out_bf, mod, lat_row, tm)
    y_sample = _peer(x1, mod, lat_row, norm2_g[l], final_g, wq_t, sk, u_bf, vt_bf, tt, ec)

    return (y_prompt.reshape(b, s, d), y_sample.reshape(bd, sd, d), state_k, state_v)
```

```python
import functools
import math

import jax
import jax.numpy as jnp
import numpy as np
from jax import lax
from jax.experimental import pallas as pl
from jax.experimental.pallas import tpu as pltpu

F32 = jnp.float32
BF16 = jnp.bfloat16

EPS = 1e-6
LANES = 128
GRID_W = 64
N_HEADS = 4
HEAD_DIM = 64
ATT_W = N_HEADS * 2 * HEAD_DIM
N_GROUPS = 4
GROUP_DIM = 128
ROPE_BASE = 10000.0
PEER_HEADS = 8
N_KEYS = 128
TOPK = 16
ROWS = 16
GATE_ROWS = 32
VMEM_LIMIT = 56 * 1024 * 1024

_CAND_CNT = [TOPK // (k + 1) for k in range(TOPK)]


def _cparams(sem):
    return pltpu.CompilerParams(dimension_semantics=sem, vmem_limit_bytes=VMEM_LIMIT)


def _mod_kernel(c_ref, w_ref, b_ref, o_ref):
    c = c_ref[...]
    a = c / (1.0 + jnp.exp(-c))
    o_ref[...] = jnp.dot(a, w_ref[...], preferred_element_type=F32,
                         precision=lax.Precision.HIGHEST) + b_ref[...]


def _modulation(cond, w_mod, b_mod):
    rows, d = cond.shape
    n = w_mod.shape[1]
    tn = 1536
    return pl.pallas_call(
        _mod_kernel,
        grid=(n // tn,),
        in_specs=[pl.BlockSpec((rows, d), lambda j: (0, 0)),
                  pl.BlockSpec((d, tn), lambda j: (0, j)),
                  pl.BlockSpec((1, tn), lambda j: (0, j))],
        out_specs=pl.BlockSpec((rows, tn), lambda j: (0, j)),
        out_shape=jax.ShapeDtypeStruct((rows, n), F32),
        compiler_params=_cparams(("arbitrary",)),
        name="modulation",
    )(cond, w_mod, b_mod.reshape(1, n))


def _inproj_kernel(*refs, rope):
    if rope:
        x_ref, mod_ref, g_ref, w_ref, cos_ref, sa_ref, sb_ref, q_ref, k_ref, v_ref, f_ref = refs
    else:
        x_ref, mod_ref, g_ref, w_ref, q_ref, k_ref, v_ref, f_ref = refs
    x = x_ref[...]
    ms = jnp.mean(x * x, axis=-1, keepdims=True)
    xn = x * lax.rsqrt(ms + EPS) * g_ref[...]
    h = xn * (1.0 + mod_ref[0, 1:2, :]) + mod_ref[0, 0:1, :]
    z = jnp.dot(h.astype(BF16), w_ref[...], preferred_element_type=F32)
    scale = HEAD_DIM ** -0.5
    if rope:
        cos, sa, sb = cos_ref[...], sa_ref[...], sb_ref[...]
        for ch in range(ATT_W // LANES):
            lo, hi = ch * LANES, (ch + 1) * LANES
            for base, ref, sc in ((0, q_ref, scale), (ATT_W, k_ref, 1.0)):
                t = z[:, base + lo:base + hi]
                r = t * cos + pltpu.roll(t, LANES - 16, 1) * sa + pltpu.roll(t, 16, 1) * sb
                ref[:, lo:hi] = (r * sc).astype(ref.dtype)
    else:
        q_ref[...] = (z[:, :ATT_W] * scale).astype(q_ref.dtype)
        k_ref[...] = z[:, ATT_W:2 * ATT_W].astype(k_ref.dtype)
    v_ref[...] = z[:, 2 * ATT_W:3 * ATT_W].astype(v_ref.dtype)
    f_ref[...] = z[:, 3 * ATT_W:].astype(f_ref.dtype)


def _inproj(x2d, mod, norm_g, w_in_bf, mod_row, tables, kv_dtype, tm):
    t, d = x2d.shape
    n = w_in_bf.shape[1]
    fw = n - 3 * ATT_W
    rope = tables is not None
    in_specs = [pl.BlockSpec((tm, d), lambda i: (i, 0)),
                pl.BlockSpec((1, 6, d), lambda i: (mod_row(i), 0, 0)),
                pl.BlockSpec((1, d), lambda i: (0, 0)),
                pl.BlockSpec((d, n), lambda i: (0, 0))]
    args = [x2d, mod, norm_g.reshape(1, d), w_in_bf]
    if rope:
        seq = tables[0].shape[0]
        per = seq // tm
        for tb in tables:
            in_specs.append(pl.BlockSpec((tm, LANES), lambda i: (i % per, 0)))
            args.append(tb)
    return pl.pallas_call(
        functools.partial(_inproj_kernel, rope=rope),
        grid=(t // tm,),
        in_specs=in_specs,
        out_specs=[pl.BlockSpec((tm, ATT_W), lambda i: (i, 0)),
                   pl.BlockSpec((tm, ATT_W), lambda i: (i, 0)),
                   pl.BlockSpec((tm, ATT_W), lambda i: (i, 0)),
                   pl.BlockSpec((tm, fw), lambda i: (i, 0))],
        out_shape=[jax.ShapeDtypeStruct((t, ATT_W), BF16),
                   jax.ShapeDtypeStruct((t, ATT_W), kv_dtype),
                   jax.ShapeDtypeStruct((t, ATT_W), kv_dtype),
                   jax.ShapeDtypeStruct((t, fw), BF16)],
        compiler_params=_cparams(("arbitrary",)),
        name="inproj_rope" if rope else "inproj",
    )(*args)


def _rope_tables(seq):
    rows = seq // GRID_W
    row = np.repeat(np.arange(rows), GRID_W).astype(np.float32)
    col = np.tile(np.arange(GRID_W), rows).astype(np.float32)
    nf = HEAD_DIM // 4
    inv = jnp.asarray(ROPE_BASE, F32) ** (-jnp.arange(nf, dtype=F32) / nf)
    ar = jnp.asarray(row)[:, None] * inv
    ac = jnp.asarray(col)[:, None] * inv
    ang = jnp.concatenate([ar, ar, ac, ac], axis=-1)
    cos = jnp.tile(jnp.cos(ang), (1, LANES // HEAD_DIM))
    sin = jnp.tile(jnp.sin(ang), (1, LANES // HEAD_DIM))
    first = (np.arange(LANES) % 32) < 16
    sa = jnp.where(first[None, :], -sin, 0.0)
    sb = jnp.where(first[None, :], 0.0, sin)
    return cos, sa, sb


def _attn_kernel(*refs, n_pieces, lam_init):
    lq1, lk1, lq2, lk2, sg_ref, q_ref = refs[:6]
    kv = refs[6:6 + 2 * n_pieces]
    o_ref = refs[6 + 2 * n_pieces]
    l1 = jnp.sum(lq1[...] * lk1[...], axis=-1, keepdims=True)
    l2 = jnp.sum(lq2[...] * lk2[...], axis=-1, keepdims=True)
    lam = jnp.exp(l1) - jnp.exp(l2) + lam_init
    lane = lax.broadcasted_iota(jnp.int32, (1, LANES), 1)
    nt = (((1,), (1,)), ((), ()))
    for h in range(N_HEADS):
        hs = slice(h * LANES, (h + 1) * LANES)
        qh = q_ref[0, :, hs]
        s0, s1, vs = [], [], []
        for p in range(n_pieces):
            kp = kv[2 * p][0, :, hs].astype(BF16)
            vs.append(kv[2 * p + 1][0, :, hs].astype(BF16))
            k0 = jnp.where(lane < HEAD_DIM, kp, jnp.zeros_like(kp))
            k1 = jnp.where(lane >= HEAD_DIM, kp, jnp.zeros_like(kp))
            s0.append(lax.dot_general(qh, k0, nt, preferred_element_type=F32))
            s1.append(lax.dot_general(qh, k1, nt, preferred_element_type=F32))
        ws = []
        for ss in (s0, s1):
            m = ss[0].max(axis=-1, keepdims=True)
            for s in ss[1:]:
                m = jnp.maximum(m, s.max(axis=-1, keepdims=True))
            es = [jnp.exp(s - m) for s in ss]
            den = es[0].sum(axis=-1, keepdims=True)
            for e in es[1:]:
                den = den + e.sum(axis=-1, keepdims=True)
            inv = 1.0 / den
            ws.append([e * inv for e in es])
        o = None
        for p in range(n_pieces):
            w = (ws[0][p] - lam * ws[1][p]).astype(BF16)
            t = jnp.dot(w, vs[p], preferred_element_type=F32)
            o = t if o is None else o + t
        ms = jnp.mean(o * o, axis=-1, keepdims=True)
        a = o * lax.rsqrt(ms + EPS) * sg_ref[...] * (1.0 - lam_init)
        o_ref[0, :, hs] = a.astype(o_ref.dtype)


def _attention(q, kvs, lams, subln_g, lam_init, tq):
    b, sq, w = q.shape
    n_pieces = len(kvs) // 2
    lam_specs = [pl.BlockSpec((1, HEAD_DIM), lambda i, j: (0, 0)) for _ in range(4)]
    in_specs = lam_specs + [pl.BlockSpec((1, LANES), lambda i, j: (0, 0)),
                            pl.BlockSpec((1, tq, w), lambda i, j: (i, j, 0))]
    for a in kvs:
        in_specs.append(pl.BlockSpec((1, a.shape[1], w), lambda i, j: (i, 0, 0)))
    return pl.pallas_call(
        functools.partial(_attn_kernel, n_pieces=n_pieces, lam_init=lam_init),
        grid=(b, sq // tq),
        in_specs=in_specs,
        out_specs=pl.BlockSpec((1, tq, w), lambda i, j: (i, j, 0)),
        out_shape=jax.ShapeDtypeStruct((b, sq, w), BF16),
        compiler_params=_cparams(("arbitrary", "arbitrary")),
        name="diff_attention_%d" % n_pieces,
    )(*[l.reshape(1, HEAD_DIM) for l in lams], subln_g.reshape(1, LANES), q, *kvs)


def _fourier_kernel(f_ref, wcs_ref, m2_ref, wf_ref, o_ref):
    for g in range(N_GROUPS):
        gs = slice(g * GROUP_DIM, (g + 1) * GROUP_DIM)
        x = f_ref[0, :, gs]
        y = jnp.dot(x, wcs_ref[...], preferred_element_type=F32)
        yc = jnp.concatenate([y[:, :GROUP_DIM], y[:, GROUP_DIM:]], axis=0).astype(BF16)
        spec = jnp.dot(m2_ref[...], yc, preferred_element_type=F32)
        o_ref[0, :, gs] = jnp.dot(spec.astype(BF16), wf_ref[g],
                                  preferred_element_type=F32).astype(o_ref.dtype)


def _dft_tables(seq):
    kc = np.arange(GROUP_DIM)
    ang_c = 2.0 * np.pi * ((kc[:, None] * kc[None, :]) % GROUP_DIM) / GROUP_DIM
    wcs = np.concatenate([np.cos(ang_c), np.sin(ang_c)], axis=1) / math.sqrt(GROUP_DIM)
    ks = np.arange(seq)
    ang_s = 2.0 * np.pi * ((ks[:, None] * ks[None, :]) % seq) / seq
    m2 = np.concatenate([np.cos(ang_s), -np.sin(ang_s)], axis=1) / math.sqrt(seq)
    return jnp.asarray(wcs, F32).astype(BF16), jnp.asarray(m2, F32).astype(BF16)


def _fourier(f, w_f_bf):
    b, s, w = f.shape
    wcs, m2 = _dft_tables(s)
    return pl.pallas_call(
        _fourier_kernel,
        grid=(b,),
        in_specs=[pl.BlockSpec((1, s, w), lambda i: (i, 0, 0)),
                  pl.BlockSpec(wcs.shape, lambda i: (0, 0)),
                  pl.BlockSpec(m2.shape, lambda i: (0, 0)),
                  pl.BlockSpec(w_f_bf.shape, lambda i: (0, 0, 0))],
        out_specs=pl.BlockSpec((1, s, w), lambda i: (i, 0, 0)),
        out_shape=jax.ShapeDtypeStruct((b, s, w), BF16),
        compiler_params=_cparams(("arbitrary",)),
        name="fourier_mix",
    )(f, wcs, m2, w_f_bf)


def _outproj_kernel(x_ref, a_ref, f_ref, wa_ref, wf_ref, mod_ref, o_ref):
    m = jnp.dot(a_ref[...], wa_ref[...], preferred_element_type=F32)
    m = m + jnp.dot(f_ref[...], wf_ref[...], preferred_element_type=F32)
    o_ref[...] = x_ref[...] + mod_ref[0, 2:3, :] * m


def _outproj(x2d, a2d, f2d, w_out_bf, mod, mod_row, tm):
    t, d = x2d.shape
    wa, wf = w_out_bf[:ATT_W], w_out_bf[ATT_W:]
    return pl.pallas_call(
        _outproj_kernel,
        grid=(t // tm,),
        in_specs=[pl.BlockSpec((tm, d), lambda i: (i, 0)),
                  pl.BlockSpec((tm, ATT_W), lambda i: (i, 0)),
                  pl.BlockSpec((tm, wf.shape[0]), lambda i: (i, 0)),
                  pl.BlockSpec(wa.shape, lambda i: (0, 0)),
                  pl.BlockSpec(wf.shape, lambda i: (0, 0)),
                  pl.BlockSpec((1, 6, d), lambda i: (mod_row(i), 0, 0))],
        out_specs=pl.BlockSpec((tm, d), lambda i: (i, 0)),
        out_shape=jax.ShapeDtypeStruct((t, d), F32),
        compiler_params=_cparams(("arbitrary",)),
        name="outproj",
    )(x2d, a2d, f2d, wa, wf, mod)


def _top16(s, ids, exact, want_rank=True):
    work = s
    rank = jnp.full(s.shape, float(TOPK), F32) if want_rank else None
    vals = []
    for r in range(TOPK):
        m = jnp.max(work, axis=0, keepdims=True)
        sel = work == m
        if exact:
            sel = ids == jnp.min(jnp.where(sel, ids, 1e9), axis=0, keepdims=True)
        if want_rank:
            rank = jnp.where(sel, float(r), rank)
        work = jnp.where(sel, -jnp.inf, work)
        vals.append(m)
    return rank, vals


def _rows_to_array(rows):
    it = lax.broadcasted_iota(jnp.int32, (TOPK, LANES), 0)
    out = jnp.zeros((TOPK, LANES), F32)
    for r, row in enumerate(rows):
        out = jnp.where(it == r, row, out)
    return out


def _candidates(v1a, v2a, exact):
    it8 = lax.broadcasted_iota(jnp.int32, (8, LANES), 0)
    it16 = lax.broadcasted_iota(jnp.int32, (TOPK, LANES), 0)
    blocks = [v1a[0:1, :] + v2a]
    idb = [it16.astype(F32)]
    for k1 in range(1, 8):
        blocks.append(jnp.where(it8 < _CAND_CNT[k1], v1a[k1:k1 + 1, :] + v2a[0:8, :], -jnp.inf))
        idb.append((it8 + k1 * TOPK).astype(F32))
    blocks.append(v1a[8:16, :] + v2a[0:1, :])
    idb.append(((it8 + 8) * TOPK).astype(F32))
    cand = jnp.concatenate(blocks, axis=0)
    cid = jnp.concatenate(idb, axis=0)
    rc, top = _top16(cand, cid, exact)
    z = jnp.zeros((1, LANES), F32)
    for r in range(TOPK):
        z = z + jnp.exp(top[r] - top[0])
    picked = jnp.where(rc < float(TOPK), 1.0, 0.0)
    cnt_lo = jnp.zeros((8, LANES), F32)
    for k1 in range(8):
        lo = 0 if k1 == 0 else 8 + 8 * k1
        n = jnp.sum(picked[lo:lo + (TOPK if k1 == 0 else 8), :], axis=0, keepdims=True)
        cnt_lo = jnp.where(it8 == k1, n, cnt_lo)
    cnt = jnp.concatenate([cnt_lo, picked[72:80, :]], axis=0)
    return cnt, z


def _count_ne16(mask_f32):
    return jnp.where(jnp.sum(mask_f32, axis=0, keepdims=True) != float(TOPK), 1.0, 0.0)


def _peer_gates_exact(s1, s2):
    ids = lax.broadcasted_iota(jnp.int32, (N_KEYS, LANES), 0).astype(F32)
    r1, v1 = _top16(s1, ids, True)
    r2, v2 = _top16(s2, ids, True)
    cnt, z = _candidates(_rows_to_array(v1), _rows_to_array(v2), True)
    c1 = jnp.zeros((N_KEYS, LANES), F32)
    for k in range(TOPK):
        c1 = jnp.where(r1 == float(k), cnt[k:k + 1, :], c1)
    w1 = jnp.exp(s1 - v1[0]) * (1.0 / z)
    e2 = jnp.exp(s2 - v2[0])
    return r2, e2, c1, w1


def _peer_kernel(x_ref, mod_ref, g2_ref, gf_ref, wq_ref, sk_ref, u_ref, vt_ref, y_ref,
                 h2t_s, qt_s, s_s, v_s, r2_s, e2_s, c1_s, w1_s, at_s, act_s, acc_s, redo_s, *, tt, ec):
    j = pl.program_id(1)
    nlg = tt // LANES
    nb = ec // N_KEYS

    @pl.when(j == 0)
    def _prologue():
        x = x_ref[...]
        ms = jnp.mean(x * x, axis=-1, keepdims=True)
        xn = x * lax.rsqrt(ms + EPS) * g2_ref[...]
        h2 = xn * (1.0 + mod_ref[0, 4:5, :]) + mod_ref[0, 3:4, :]
        h2t_s[...] = h2.T.astype(BF16)

        qt_s[...] = jnp.dot(wq_ref[...], h2t_s[...], preferred_element_type=F32).astype(BF16)
        for blk in range(2 * PEER_HEADS):
            s_s[blk] = jnp.dot(sk_ref[blk], qt_s[blk * N_KEYS:(blk + 1) * N_KEYS, :],
                               preferred_element_type=F32)

        n_items = PEER_HEADS * nlg
        want = float(TOPK)

        def scores(idx):
            h, lg = idx // nlg, idx % nlg
            c0 = pl.multiple_of(lg * LANES, LANES)
            return h, lg, s_s[2 * h, :, pl.ds(c0, LANES)], s_s[2 * h + 1, :, pl.ds(c0, LANES)]

        def store_i1(h, lg, c1, w1):
            c1_s[lg, pl.ds(h, N_KEYS, stride=PEER_HEADS), :] = c1
            w1_s[lg, pl.ds(h, N_KEYS, stride=PEER_HEADS), :] = w1

        def sides(idx, prev_bad):
            redo_s[idx] = (jnp.max(prev_bad) > 0.0).astype(jnp.int32)
            h, lg, s1, s2 = scores(idx)
            _, v1 = _top16(s1, None, False, want_rank=False)
            r2, v2 = _top16(s2, None, False)
            r2_s[h, lg] = r2.astype(BF16)
            e2_s[h, lg] = jnp.exp(s2 - v2[0]).astype(BF16)
            v_s[idx, 0:TOPK, :] = _rows_to_array(v1)
            v_s[idx, TOPK:2 * TOPK, :] = _rows_to_array(v2)
            return (_count_ne16(jnp.where(s1 >= v1[TOPK - 1], 1.0, 0.0))
                    + _count_ne16(jnp.where(r2 < want, 1.0, 0.0)))
        last_bad = lax.fori_loop(0, n_items, sides, jnp.zeros((1, LANES), F32))
        redo_s[n_items] = (jnp.max(last_bad) > 0.0).astype(jnp.int32)

        def cells(pair, carry):
            for t in range(2):
                idx = 2 * pair + t
                h, lg, s1, _ = scores(idx)
                v1a = v_s[idx, 0:TOPK, :]
                cnt, z = _candidates(v1a, v_s[idx, TOPK:2 * TOPK, :], False)
                c1 = jnp.zeros((N_KEYS, LANES), F32)
                for k in range(TOPK):
                    c1 = jnp.where(s1 == v1a[k:k + 1, :], cnt[k:k + 1, :], c1)
                store_i1(h, lg, c1, jnp.exp(s1 - v1a[0:1, :]) * (1.0 / z))
                redo_s[idx + 1] = redo_s[idx + 1] + (jnp.max(_count_ne16(cnt)) > 0.0).astype(jnp.int32)
            return carry
        lax.fori_loop(0, n_items // 2, cells, 0)

        def redo(idx, carry):
            @pl.when(redo_s[idx + 1] > 0)
            def _ties():
                h, lg, s1, s2 = scores(idx)
                r2, e2, c1, w1 = _peer_gates_exact(s1, s2)
                r2_s[h, lg] = r2.astype(BF16)
                e2_s[h, lg] = e2.astype(BF16)
                store_i1(h, lg, c1, w1)
            return carry
        lax.fori_loop(0, n_items, redo, 0)

        acc_s[...] = jnp.zeros_like(acc_s)

    at_s[:, 0:tt] = jnp.dot(u_ref[...], h2t_s[...], preferred_element_type=F32)

    n_sub = N_KEYS // GATE_ROWS

    def gate_trip(it, carry):
        l = it // n_sub
        row0 = (it % n_sub) * GATE_ROWS
        c0 = pl.multiple_of(l * LANES, LANES)
        cb, wb = [], []
        for b in range(nb):
            i1 = pl.multiple_of((j * nb + b) * PEER_HEADS, PEER_HEADS)
            cb.append(c1_s[l, pl.ds(i1, PEER_HEADS), :].astype(BF16))
            wb.append(w1_s[l, pl.ds(i1, PEER_HEADS), :].astype(BF16))
        subs = range(GATE_ROWS // ROWS)
        g = [[jnp.zeros((ROWS, LANES), BF16) for _ in range(nb)] for _ in subs]
        zero = jnp.zeros((), BF16)
        for h in range(PEER_HEADS):
            for s in subs:
                rows = pl.ds(pl.multiple_of(row0 + s * ROWS, ROWS), ROWS)
                r2t, e2t = r2_s[h, l, rows, :], e2_s[h, l, rows, :]
                for b in range(nb):
                    g[s][b] = g[s][b] + e2t * jnp.where(r2t < cb[b][h:h + 1, :], wb[b][h:h + 1, :], zero)
        for s in subs:
            for b in range(nb):
                rows = pl.ds(pl.multiple_of(b * N_KEYS + row0 + s * ROWS, ROWS), ROWS)
                a = at_s[rows, pl.ds(c0, LANES)]
                gelu = 0.5 * a * (1.0 + lax.erf(a * (2.0 ** -0.5)))
                act_s[rows, pl.ds(c0, LANES)] = g[s][b] * gelu.astype(BF16)
        return carry
    lax.fori_loop(0, nlg * n_sub, gate_trip, 0)

    acc_s[:, 0:tt] += jnp.dot(vt_ref[0], act_s[:, 0:tt], preferred_element_type=F32)

    @pl.when(j == pl.num_programs(1) - 1)
    def _epilogue():
        x2 = x_ref[...] + mod_ref[0, 5:6, :] * acc_s[:, 0:tt].T
        ms = jnp.mean(x2 * x2, axis=-1, keepdims=True)
        y_ref[...] = x2 * lax.rsqrt(ms + EPS) * gf_ref[...]


def _peer(x1, mod, mod_row, norm2_g, final_g, wq_t, sk, u_bf, vt_bf, tt, ec):
    t, d = x1.shape
    ne = u_bf.shape[0]
    scratch = [pltpu.VMEM((d, tt), BF16),
               pltpu.VMEM((wq_t.shape[0], tt), BF16),
               pltpu.VMEM((2 * PEER_HEADS, N_KEYS, tt), F32),
               pltpu.VMEM((PEER_HEADS * tt // LANES, 2 * TOPK, LANES), F32),
               pltpu.VMEM((PEER_HEADS, tt // LANES, N_KEYS, LANES), BF16),
               pltpu.VMEM((PEER_HEADS, tt // LANES, N_KEYS, LANES), BF16),
               pltpu.VMEM((tt // LANES, N_KEYS * PEER_HEADS, LANES), F32),
               pltpu.VMEM((tt // LANES, N_KEYS * PEER_HEADS, LANES), F32),
               pltpu.VMEM((ec, tt + LANES), F32),
               pltpu.VMEM((ec, tt + LANES), BF16),
               pltpu.VMEM((d, tt + LANES), F32),
               pltpu.SMEM((PEER_HEADS * tt // LANES + 1,), jnp.int32)]
    return pl.pallas_call(
        functools.partial(_peer_kernel, tt=tt, ec=ec),
        grid=(t // tt, ne // ec),
        in_specs=[pl.BlockSpec((tt, d), lambda i, j: (i, 0)),
                  pl.BlockSpec((1, 6, d), lambda i, j: (mod_row(i), 0, 0)),
                  pl.BlockSpec((1, d), lambda i, j: (0, 0)),
                  pl.BlockSpec((1, d), lambda i, j: (0, 0)),
                  pl.BlockSpec(wq_t.shape, lambda i, j: (0, 0)),
                  pl.BlockSpec(sk.shape, lambda i, j: (0, 0, 0)),
                  pl.BlockSpec((ec, d), lambda i, j: (j, 0)),
                  pl.BlockSpec((1, d, ec), lambda i, j: (j, 0, 0))],
        out_specs=pl.BlockSpec((tt, d), lambda i, j: (i, 0)),
        out_shape=jax.ShapeDtypeStruct((t, d), F32),
        scratch_shapes=scratch,
        compiler_params=_cparams(("arbitrary", "arbitrary")),
        name="peer_dense",
    )(x1, mod, norm2_g.reshape(1, d), final_g.reshape(1, d), wq_t, sk, u_bf, vt_bf)


def kernel(x_prompt, x_sample, cache_k, cache_v, c, c_ctx, w_mod, b_mod, norm1_g, w_in, lam_q1, lam_k1,
           lam_q2, lam_k2, subln_g, w_fourier, w_out, norm2_g, w_query, sub_keys, expert_u, expert_v,
           final_g):
    b, s, d = x_prompt.shape
    bd, sd, _ = x_sample.shape
    p = cache_k.shape[2]
    depth = w_mod.shape[0]
    assert depth == 1
    tm = 512
    tt, ec = 512, 1024

    rows = 16
    cond = jnp.zeros((rows, d), F32).at[0].set(c_ctx).at[1:1 + bd].set(c)
    tables = _rope_tables(sd)

    xp = x_prompt.reshape(b * s, d)
    xs = x_sample.reshape(bd * sd, d)
    l = 0
    lam_init = 0.8 - 0.6 * math.exp(-0.3 * l)
    mod = _modulation(cond, w_mod[l], b_mod[l]).reshape(rows, 6, d)
    w_in_bf = w_in[l].astype(BF16)
    w_out_bf = w_out[l].astype(BF16)
    w_f_bf = w_fourier[l].astype(BF16)
    wq_t = w_query[l].T.astype(BF16)
    sk = sub_keys[l].reshape(2 * PEER_HEADS, N_KEYS, -1).astype(BF16)
    u_bf = expert_u[l].astype(BF16)
    vt_bf = expert_v[l].reshape(-1, ec, d).transpose(0, 2, 1).astype(BF16)
    lams = (lam_q1[l], lam_k1[l], lam_q2[l], lam_k2[l])

    ctx_row = lambda i: 0
    lat_row = lambda i: 1 + i // (sd // tm)

    q, k, v, f = _inproj(xp, mod, norm1_g[l], w_in_bf, ctx_row, None, F32, tm)
    k3, v3 = k.reshape(b, s, ATT_W), v.reshape(b, s, ATT_W)
    att = _attention(q.reshape(b, s, ATT_W), (k3, v3), lams, subln_g[l], lam_init, s)
    four = _fourier(f.reshape(b, s, -1), w_f_bf)
    x1 = _outproj(xp, att.reshape(b * s, ATT_W), four.reshape(b * s, -1), w_out_bf, mod, ctx_row, tm)
    y_prompt = _peer(x1, mod, ctx_row, norm2_g[l], final_g, wq_t, sk, u_bf, vt_bf, tt, ec)
    state_k = k.reshape(b, 1, s, N_HEADS, 2 * HEAD_DIM)
    state_v = v.reshape(b, 1, s, N_HEADS, 2 * HEAD_DIM)

    q, k, v, f = _inproj(xs, mod, norm1_g[l], w_in_bf, lat_row, tables, BF16, tm)
    ck = cache_k[:, l].reshape(bd, p, ATT_W)
    cv = cache_v[:, l].reshape(bd, p, ATT_W)
    att = _attention(q.reshape(bd, sd, ATT_W),
                     (ck, cv, k.reshape(bd, sd, ATT_W), v.reshape(bd, sd, ATT_W)),
                     lams, subln_g[l], lam_init, 256)
    four = _fourier(f.reshape(bd, sd, -1), w_f_bf)
    x1 = _outproj(xs, att.reshape(bd * sd, ATT_W), four.reshape(bd * sd, -1), w_out_bf, mod, lat_row, tm)
    y_sample = _peer(x1, mod, lat_row, norm2_g[l], final_g, wq_t, sk, u_bf, vt_bf, tt, ec)

    return (y_prompt.reshape(b, s, d), y_sample.reshape(bd, sd, d), state_k, state_v)
```

```python
import functools
import math

import jax
import jax.numpy as jnp
import numpy as np
from jax import lax
from jax.experimental import pallas as pl
from jax.experimental.pallas import tpu as pltpu

F32 = jnp.float32
BF16 = jnp.bfloat16

EPS = 1e-6
LANES = 128
GRID_W = 64
N_HEADS = 4
HEAD_DIM = 64
ATT_W = N_HEADS * 2 * HEAD_DIM
N_GROUPS = 4
GROUP_DIM = 128
ROPE_BASE = 10000.0
PEER_HEADS = 8
N_KEYS = 128
TOPK = 16
ROWS = 16
GATE_ROWS = 32
VMEM_LIMIT = 56 * 1024 * 1024

_CAND_CNT = [TOPK // (k + 1) for k in range(TOPK)]


def _cparams(sem):
    return pltpu.CompilerParams(dimension_semantics=sem, vmem_limit_bytes=VMEM_LIMIT)


def _mod_kernel(c_ref, w_ref, b_ref, o_ref):
    c = c_ref[...]
    a = c / (1.0 + jnp.exp(-c))
    o_ref[...] = jnp.dot(a.astype(BF16), w_ref[...].astype(BF16), preferred_element_type=F32) + b_ref[...]


def _modulation(cond, w_mod, b_mod):
    rows, d = cond.shape
    n = w_mod.shape[1]
    tn = 1536
    return pl.pallas_call(
        _mod_kernel,
        grid=(n // tn,),
        in_specs=[pl.BlockSpec((rows, d), lambda j: (0, 0)),
                  pl.BlockSpec((d, tn), lambda j: (0, j)),
                  pl.BlockSpec((1, tn), lambda j: (0, j))],
        out_specs=pl.BlockSpec((rows, tn), lambda j: (0, j)),
        out_shape=jax.ShapeDtypeStruct((rows, n), F32),
        compiler_params=_cparams(("arbitrary",)),
        name="modulation",
    )(cond, w_mod, b_mod.reshape(1, n))


def _inproj_kernel(*refs, rope):
    if rope:
        x_ref, mod_ref, g_ref, w_ref, cos_ref, sa_ref, sb_ref, q_ref, k_ref, v_ref, f_ref = refs
    else:
        x_ref, mod_ref, g_ref, w_ref, q_ref, k_ref, v_ref, f_ref = refs
    x = x_ref[...]
    ms = jnp.mean(x * x, axis=-1, keepdims=True)
    xn = x * lax.rsqrt(ms + EPS) * g_ref[...]
    h = xn * (1.0 + mod_ref[0, 1:2, :]) + mod_ref[0, 0:1, :]
    z = jnp.dot(h.astype(BF16), w_ref[...], preferred_element_type=F32)
    scale = HEAD_DIM ** -0.5
    if rope:
        cos, sa, sb = cos_ref[...], sa_ref[...], sb_ref[...]
        for ch in range(ATT_W // LANES):
            lo, hi = ch * LANES, (ch + 1) * LANES
            for base, ref, sc in ((0, q_ref, scale), (ATT_W, k_ref, 1.0)):
                t = z[:, base + lo:base + hi]
                r = t * cos + pltpu.roll(t, LANES - 16, 1) * sa + pltpu.roll(t, 16, 1) * sb
                ref[:, lo:hi] = (r * sc).astype(ref.dtype)
    else:
        q_ref[...] = (z[:, :ATT_W] * scale).astype(q_ref.dtype)
        k_ref[...] = z[:, ATT_W:2 * ATT_W].astype(k_ref.dtype)
    v_ref[...] = z[:, 2 * ATT_W:3 * ATT_W].astype(v_ref.dtype)
    f_ref[...] = z[:, 3 * ATT_W:].astype(f_ref.dtype)


def _inproj(x2d, mod, norm_g, w_in_bf, mod_row, tables, kv_dtype, tm):
    t, d = x2d.shape
    n = w_in_bf.shape[1]
    fw = n - 3 * ATT_W
    rope = tables is not None
    in_specs = [pl.BlockSpec((tm, d), lambda i: (i, 0)),
                pl.BlockSpec((1, 6, d), lambda i: (mod_row(i), 0, 0)),
                pl.BlockSpec((1, d), lambda i: (0, 0)),
                pl.BlockSpec((d, n), lambda i: (0, 0))]
    args = [x2d, mod, norm_g.reshape(1, d), w_in_bf]
    if rope:
        seq = tables[0].shape[0]
        per = seq // tm
        for tb in tables:
            in_specs.append(pl.BlockSpec((tm, LANES), lambda i: (i % per, 0)))
            args.append(tb)
    return pl.pallas_call(
        functools.partial(_inproj_kernel, rope=rope),
        grid=(t // tm,),
        in_specs=in_specs,
        out_specs=[pl.BlockSpec((tm, ATT_W), lambda i: (i, 0)),
                   pl.BlockSpec((tm, ATT_W), lambda i: (i, 0)),
                   pl.BlockSpec((tm, ATT_W), lambda i: (i, 0)),
                   pl.BlockSpec((tm, fw), lambda i: (i, 0))],
        out_shape=[jax.ShapeDtypeStruct((t, ATT_W), BF16),
                   jax.ShapeDtypeStruct((t, ATT_W), kv_dtype),
                   jax.ShapeDtypeStruct((t, ATT_W), kv_dtype),
                   jax.ShapeDtypeStruct((t, fw), BF16)],
        compiler_params=_cparams(("arbitrary",)),
        name="inproj_rope" if rope else "inproj",
    )(*args)


def _rope_tables(seq):
    rows = seq // GRID_W
    row = np.repeat(np.arange(rows), GRID_W).astype(np.float32)
    col = np.tile(np.arange(GRID_W), rows).astype(np.float32)
    nf = HEAD_DIM // 4
    inv = jnp.asarray(ROPE_BASE, F32) ** (-jnp.arange(nf, dtype=F32) / nf)
    ar = jnp.asarray(row)[:, None] * inv
    ac = jnp.asarray(col)[:, None] * inv
    ang = jnp.concatenate([ar, ar, ac, ac], axis=-1)
    cos = jnp.tile(jnp.cos(ang), (1, LANES // HEAD_DIM))
    sin = jnp.tile(jnp.sin(ang), (1, LANES // HEAD_DIM))
    first = (np.arange(LANES) % 32) < 16
    sa = jnp.where(first[None, :], -sin, 0.0)
    sb = jnp.where(first[None, :], 0.0, sin)
    return cos, sa, sb


def _attn_kernel(*refs, n_pieces, lam_init):
    lq1, lk1, lq2, lk2, sg_ref, q_ref = refs[:6]
    kv = refs[6:6 + 2 * n_pieces]
    o_ref = refs[6 + 2 * n_pieces]
    l1 = jnp.sum(lq1[...] * lk1[...], axis=-1, keepdims=True)
    l2 = jnp.sum(lq2[...] * lk2[...], axis=-1, keepdims=True)
    lam = jnp.exp(l1) - jnp.exp(l2) + lam_init
    lane = lax.broadcasted_iota(jnp.int32, (1, LANES), 1)
    nt = (((1,), (1,)), ((), ()))
    for h in range(N_HEADS):
        hs = slice(h * LANES, (h + 1) * LANES)
        qh = q_ref[0, :, hs]
        s0, s1, vs = [], [], []
        for p in range(n_pieces):
            kp = kv[2 * p][0, :, hs].astype(BF16)
            vs.append(kv[2 * p + 1][0, :, hs].astype(BF16))
            k0 = jnp.where(lane < HEAD_DIM, kp, jnp.zeros_like(kp))
            k1 = jnp.where(lane >= HEAD_DIM, kp, jnp.zeros_like(kp))
            s0.append(lax.dot_general(qh, k0, nt, preferred_element_type=F32))
            s1.append(lax.dot_general(qh, k1, nt, preferred_element_type=F32))
        outs = []
        for ss in (s0, s1):
            m = ss[0].max(axis=-1, keepdims=True)
            for s in ss[1:]:
                m = jnp.maximum(m, s.max(axis=-1, keepdims=True))
            den, num = None, None
            for s, vp in zip(ss, vs):
                e = jnp.exp(s - m)
                d = e.sum(axis=-1, keepdims=True)
                t = jnp.dot(e.astype(BF16), vp, preferred_element_type=F32)
                den = d if den is None else den + d
                num = t if num is None else num + t
            outs.append(num * (1.0 / den))
        o = outs[0] - lam * outs[1]
        ms = jnp.mean(o * o, axis=-1, keepdims=True)
        a = o * lax.rsqrt(ms + EPS) * sg_ref[...] * (1.0 - lam_init)
        o_ref[0, :, hs] = a.astype(o_ref.dtype)


def _attention(q, kvs, lams, subln_g, lam_init, tq):
    b, sq, w = q.shape
    n_pieces = len(kvs) // 2
    lam_specs = [pl.BlockSpec((1, HEAD_DIM), lambda i, j: (0, 0)) for _ in range(4)]
    in_specs = lam_specs + [pl.BlockSpec((1, LANES), lambda i, j: (0, 0)),
                            pl.BlockSpec((1, tq, w), lambda i, j: (i, j, 0))]
    for a in kvs:
        in_specs.append(pl.BlockSpec((1, a.shape[1], w), lambda i, j: (i, 0, 0)))
    return pl.pallas_call(
        functools.partial(_attn_kernel, n_pieces=n_pieces, lam_init=lam_init),
        grid=(b, sq // tq),
        in_specs=in_specs,
        out_specs=pl.BlockSpec((1, tq, w), lambda i, j: (i, j, 0)),
        out_shape=jax.ShapeDtypeStruct((b, sq, w), BF16),
        compiler_params=_cparams(("arbitrary", "arbitrary")),
        name="diff_attention_%d" % n_pieces,
    )(*[l.reshape(1, HEAD_DIM) for l in lams], subln_g.reshape(1, LANES), q, *kvs)


def _fourier_kernel(f_ref, wcs_ref, m2_ref, wf_ref, o_ref):
    gslices = [slice(g * GROUP_DIM, (g + 1) * GROUP_DIM) for g in range(N_GROUPS)]
    ys = [jnp.dot(f_ref[0, :, gs], wcs_ref[...], preferred_element_type=F32) for gs in gslices]
    yc = jnp.concatenate([jnp.concatenate([y[:, :GROUP_DIM] for y in ys], axis=1),
                          jnp.concatenate([y[:, GROUP_DIM:] for y in ys], axis=1)], axis=0).astype(BF16)
    spec = jnp.dot(m2_ref[...], yc, preferred_element_type=F32).astype(BF16)
    for g, gs in enumerate(gslices):
        o_ref[0, :, gs] = jnp.dot(spec[:, gs], wf_ref[g], preferred_element_type=F32).astype(o_ref.dtype)


def _dft_tables(seq):
    kc = np.arange(GROUP_DIM)
    ang_c = 2.0 * np.pi * ((kc[:, None] * kc[None, :]) % GROUP_DIM) / GROUP_DIM
    wcs = np.concatenate([np.cos(ang_c), np.sin(ang_c)], axis=1) / math.sqrt(GROUP_DIM)
    ks = np.arange(seq)
    ang_s = 2.0 * np.pi * ((ks[:, None] * ks[None, :]) % seq) / seq
    m2 = np.concatenate([np.cos(ang_s), -np.sin(ang_s)], axis=1) / math.sqrt(seq)
    return jnp.asarray(wcs, F32).astype(BF16), jnp.asarray(m2, F32).astype(BF16)


def _fourier(f, w_f_bf):
    b, s, w = f.shape
    wcs, m2 = _dft_tables(s)
    return pl.pallas_call(
        _fourier_kernel,
        grid=(b,),
        in_specs=[pl.BlockSpec((1, s, w), lambda i: (i, 0, 0)),
                  pl.BlockSpec(wcs.shape, lambda i: (0, 0)),
                  pl.BlockSpec(m2.shape, lambda i: (0, 0)),
                  pl.BlockSpec(w_f_bf.shape, lambda i: (0, 0, 0))],
        out_specs=pl.BlockSpec((1, s, w), lambda i: (i, 0, 0)),
        out_shape=jax.ShapeDtypeStruct((b, s, w), BF16),
        compiler_params=_cparams(("arbitrary",)),
        name="fourier_mix",
    )(f, wcs, m2, w_f_bf)


def _outproj_kernel(x_ref, a_ref, f_ref, wa_ref, wf_ref, mod_ref, o_ref):
    m = jnp.dot(a_ref[...], wa_ref[...], preferred_element_type=F32)
    m = m + jnp.dot(f_ref[...], wf_ref[...], preferred_element_type=F32)
    o_ref[...] = x_ref[...] + mod_ref[0, 2:3, :] * m


def _outproj(x2d, a2d, f2d, w_out_bf, mod, mod_row, tm):
    t, d = x2d.shape
    wa, wf = w_out_bf[:ATT_W], w_out_bf[ATT_W:]
    return pl.pallas_call(
        _outproj_kernel,
        grid=(t // tm,),
        in_specs=[pl.BlockSpec((tm, d), lambda i: (i, 0)),
                  pl.BlockSpec((tm, ATT_W), lambda i: (i, 0)),
                  pl.BlockSpec((tm, wf.shape[0]), lambda i: (i, 0)),
                  pl.BlockSpec(wa.shape, lambda i: (0, 0)),
                  pl.BlockSpec(wf.shape, lambda i: (0, 0)),
                  pl.BlockSpec((1, 6, d), lambda i: (mod_row(i), 0, 0))],
        out_specs=pl.BlockSpec((tm, d), lambda i: (i, 0)),
        out_shape=jax.ShapeDtypeStruct((t, d), F32),
        compiler_params=_cparams(("arbitrary",)),
        name="outproj",
    )(x2d, a2d, f2d, wa, wf, mod)


def _top16(s, ids, exact, want_rank=True):
    work = s
    rank = jnp.full(s.shape, float(TOPK), F32) if want_rank else None
    vals = []
    for r in range(TOPK):
        m = jnp.max(work, axis=0, keepdims=True)
        sel = work == m
        if exact:
            sel = ids == jnp.min(jnp.where(sel, ids, 1e9), axis=0, keepdims=True)
        if want_rank:
            rank = jnp.where(sel, float(r), rank)
        work = jnp.where(sel, -jnp.inf, work)
        vals.append(m)
    return rank, vals


def _rows_to_array(rows):
    it = lax.broadcasted_iota(jnp.int32, (TOPK, LANES), 0)
    out = jnp.zeros((TOPK, LANES), F32)
    for r, row in enumerate(rows):
        out = jnp.where(it == r, row, out)
    return out


def _candidates(v1a, v2a, exact):
    it8 = lax.broadcasted_iota(jnp.int32, (8, LANES), 0)
    it16 = lax.broadcasted_iota(jnp.int32, (TOPK, LANES), 0)
    blocks = [v1a[0:1, :] + v2a]
    idb = [it16.astype(F32)]
    for k1 in range(1, 8):
        blocks.append(jnp.where(it8 < _CAND_CNT[k1], v1a[k1:k1 + 1, :] + v2a[0:8, :], -jnp.inf))
        idb.append((it8 + k1 * TOPK).astype(F32))
    blocks.append(v1a[8:16, :] + v2a[0:1, :])
    idb.append(((it8 + 8) * TOPK).astype(F32))
    cand = jnp.concatenate(blocks, axis=0)
    cid = jnp.concatenate(idb, axis=0)
    rc, top = _top16(cand, cid, exact)
    z = jnp.zeros((1, LANES), F32)
    for r in range(TOPK):
        z = z + jnp.exp(top[r] - top[0])
    picked = jnp.where(rc < float(TOPK), 1.0, 0.0)
    cnt_lo = jnp.zeros((8, LANES), F32)
    for k1 in range(8):
        lo = 0 if k1 == 0 else 8 + 8 * k1
        n = jnp.sum(picked[lo:lo + (TOPK if k1 == 0 else 8), :], axis=0, keepdims=True)
        cnt_lo = jnp.where(it8 == k1, n, cnt_lo)
    cnt = jnp.concatenate([cnt_lo, picked[72:80, :]], axis=0)
    return cnt, z


def _count_ne16(mask_f32):
    return jnp.where(jnp.sum(mask_f32, axis=0, keepdims=True) != float(TOPK), 1.0, 0.0)


def _peer_gates_exact(s1, s2):
    ids = lax.broadcasted_iota(jnp.int32, (N_KEYS, LANES), 0).astype(F32)
    r1, v1 = _top16(s1, ids, True)
    r2, v2 = _top16(s2, ids, True)
    cnt, z = _candidates(_rows_to_array(v1), _rows_to_array(v2), True)
    c1 = jnp.zeros((N_KEYS, LANES), F32)
    for k in range(TOPK):
        c1 = jnp.where(r1 == float(k), cnt[k:k + 1, :], c1)
    w1 = jnp.exp(s1 - v1[0]) * (1.0 / z)
    e2 = jnp.exp(s2 - v2[0])
    return r2, e2, c1, w1


def _peer_kernel(x_ref, xn_ref, mod_ref, modn_ref, g2_ref, gf_ref, wq_ref, sk_ref, u_ref, vt_ref, y_ref,
                 h2t_s, s_s, v_s, r2_s, e2_s, c1_s, w1_s, at_s, act_s, acc_s, redo_s, *, tt, ec):
    i, j = pl.program_id(0), pl.program_id(1)
    nlg = tt // LANES
    nb = ec // N_KEYS
    n_items = PEER_HEADS * nlg
    cur = i % 2
    nxt = 1 - cur
    want = float(TOPK)

    def project(xr, modr, par):
        x = xr[...]
        ms = jnp.mean(x * x, axis=-1, keepdims=True)
        xn = x * lax.rsqrt(ms + EPS) * g2_ref[...]
        h2 = xn * (1.0 + modr[0, 4:5, :]) + modr[0, 3:4, :]
        h2t_s[par] = h2.T.astype(BF16)
        per = ec // N_KEYS
        for part in range(2 * PEER_HEADS // per):
            act_s[:, 0:tt] = jnp.dot(wq_ref[part * ec:(part + 1) * ec, :], h2t_s[par],
                                     preferred_element_type=F32).astype(BF16)
            for k in range(per):
                s_s[part * per + k] = jnp.dot(sk_ref[part * per + k], act_s[k * N_KEYS:(k + 1) * N_KEYS, 0:tt],
                                              preferred_element_type=F32)

    def scores(idx):
        h, lg = idx // nlg, idx % nlg
        c0 = pl.multiple_of(lg * LANES, LANES)
        return h, lg, s_s[2 * h, :, pl.ds(c0, LANES)], s_s[2 * h + 1, :, pl.ds(c0, LANES)]

    def store_i2(par, h, lg, r2, e2):
        r2_s[par * PEER_HEADS + h, lg] = r2.astype(BF16)
        e2_s[par * PEER_HEADS + h, lg] = e2.astype(BF16)

    def store_i1(par, h, lg, c1, w1):
        c1_s[par * nlg + lg, pl.ds(h, N_KEYS, stride=PEER_HEADS), :] = c1
        w1_s[par * nlg + lg, pl.ds(h, N_KEYS, stride=PEER_HEADS), :] = w1

    def flag(row):
        return (jnp.max(row) > 0.0).astype(jnp.int32)

    def sides(idx, par, dep):
        h, lg, s1, s2 = scores(idx)
        _, v1 = _top16(s1 + dep, None, False, want_rank=False)
        r2, v2 = _top16(s2, None, False)
        store_i2(par, h, lg, r2, jnp.exp(s2 - v2[0]))
        v_s[idx, 0:TOPK, :] = _rows_to_array(v1)
        v_s[idx, TOPK:2 * TOPK, :] = _rows_to_array(v2)
        return (_count_ne16(jnp.where(s1 >= v1[TOPK - 1], 1.0, 0.0))
                + _count_ne16(jnp.where(r2 < want, 1.0, 0.0)))

    def cells(idx, par):
        h, lg, s1, _ = scores(idx)
        v1a = v_s[idx, 0:TOPK, :]
        cnt, z = _candidates(v1a, v_s[idx, TOPK:2 * TOPK, :], False)
        c1 = jnp.zeros((N_KEYS, LANES), F32)
        for k in range(TOPK):
            c1 = jnp.where(s1 == v1a[k:k + 1, :], cnt[k:k + 1, :], c1)
        store_i1(par, h, lg, c1, jnp.exp(s1 - v1a[0:1, :]) * (1.0 / z))
        redo_s[idx + 1] = redo_s[idx + 1] + flag(_count_ne16(cnt))

    @pl.when(j == 0)
    def _prologue():
        @pl.when(i == 0)
        def _first_tile():
            project(x_ref, mod_ref, 0)

            def side_trip(idx, prev_bad):
                redo_s[idx] = flag(prev_bad)
                return sides(idx, 0, jnp.zeros((1, LANES), F32))
            redo_s[n_items] = flag(lax.fori_loop(0, n_items, side_trip, jnp.zeros((1, LANES), F32)))

            def cell_trip(pair, carry):
                cells(2 * pair, 0)
                cells(2 * pair + 1, 0)
                return carry
            lax.fori_loop(0, n_items // 2, cell_trip, 0)

        def redo(idx, carry):
            @pl.when(redo_s[idx + 1] > 0)
            def _ties():
                h, lg, s1, s2 = scores(idx)
                r2, e2, c1, w1 = _peer_gates_exact(s1, s2)
                store_i2(cur, h, lg, r2, e2)
                store_i1(cur, h, lg, c1, w1)
            return carry
        lax.fori_loop(0, n_items, redo, 0)

        project(xn_ref, modn_ref, nxt)
        acc_s[...] = jnp.zeros_like(acc_s)

    at_s[:, 0:tt] = jnp.dot(u_ref[...], h2t_s[cur], preferred_element_type=F32)
    bad = sides(2 * j, nxt, jnp.zeros((1, LANES), F32))
    redo_s[2 * j + 1] = flag(bad)
    redo_s[2 * j + 2] = flag(sides(2 * j + 1, nxt, jnp.minimum(bad, 0.0)))

    n_sub = N_KEYS // GATE_ROWS

    def gate_trip(it, carry):
        l = it // n_sub
        row0 = (it % n_sub) * GATE_ROWS
        c0 = pl.multiple_of(l * LANES, LANES)
        cb, wb = [], []
        for b in range(nb):
            i1 = pl.multiple_of((j * nb + b) * PEER_HEADS, PEER_HEADS)
            cb.append(c1_s[cur * nlg + l, pl.ds(i1, PEER_HEADS), :].astype(BF16))
            wb.append(w1_s[cur * nlg + l, pl.ds(i1, PEER_HEADS), :].astype(BF16))
        subs = range(GATE_ROWS // ROWS)
        g = [[jnp.zeros((ROWS, LANES), BF16) for _ in range(nb)] for _ in subs]
        zero = jnp.zeros((), BF16)
        for h in range(PEER_HEADS):
            for s in subs:
                rows = pl.ds(pl.multiple_of(row0 + s * ROWS, ROWS), ROWS)
                r2t = r2_s[cur * PEER_HEADS + h, l, rows, :]
                e2t = e2_s[cur * PEER_HEADS + h, l, rows, :]
                for b in range(nb):
                    g[s][b] = g[s][b] + e2t * jnp.where(r2t < cb[b][h:h + 1, :], wb[b][h:h + 1, :], zero)
        for s in subs:
            for b in range(nb):
                rows = pl.ds(pl.multiple_of(b * N_KEYS + row0 + s * ROWS, ROWS), ROWS)
                a = at_s[rows, pl.ds(c0, LANES)]
                gelu = 0.5 * a * (1.0 + lax.erf(a * (2.0 ** -0.5)))
                act_s[rows, pl.ds(c0, LANES)] = g[s][b] * gelu.astype(BF16)
        return carry
    lax.fori_loop(0, nlg * n_sub, gate_trip, 0)

    acc_s[:, 0:tt] += jnp.dot(vt_ref[0], act_s[:, 0:tt], preferred_element_type=F32)
    cells(2 * j, nxt)
    cells(2 * j + 1, nxt)

    @pl.when(j == pl.num_programs(1) - 1)
    def _epilogue():
        x2 = x_ref[...] + mod_ref[0, 5:6, :] * acc_s[:, 0:tt].T
        ms = jnp.mean(x2 * x2, axis=-1, keepdims=True)
        y_ref[...] = x2 * lax.rsqrt(ms + EPS) * gf_ref[...]


def _peer(x1, mod, mod_row, norm2_g, final_g, wq_t, sk, u_bf, vt_bf, tt, ec):
    t, d = x1.shape
    ne = u_bf.shape[0]
    nt, nc, nlg = t // tt, ne // ec, tt // LANES
    assert PEER_HEADS * nlg == 2 * nc
    scratch = [pltpu.VMEM((2, d, tt), BF16),
               pltpu.VMEM((2 * PEER_HEADS, N_KEYS, tt), F32),
               pltpu.VMEM((PEER_HEADS * nlg, 2 * TOPK, LANES), F32),
               pltpu.VMEM((2 * PEER_HEADS, nlg, N_KEYS, LANES), BF16),
               pltpu.VMEM((2 * PEER_HEADS, nlg, N_KEYS, LANES), BF16),
               pltpu.VMEM((2 * nlg, N_KEYS * PEER_HEADS, LANES), F32),
               pltpu.VMEM((2 * nlg, N_KEYS * PEER_HEADS, LANES), F32),
               pltpu.VMEM((ec, tt + LANES), F32),
               pltpu.VMEM((ec, tt + LANES), BF16),
               pltpu.VMEM((d, tt + LANES), F32),
               pltpu.SMEM((PEER_HEADS * nlg + 1,), jnp.int32)]
    once = pl.Buffered(1)
    nxt = lambda i: jnp.minimum(i + 1, nt - 1)
    return pl.pallas_call(
        functools.partial(_peer_kernel, tt=tt, ec=ec),
        grid=(nt, nc),
        in_specs=[pl.BlockSpec((tt, d), lambda i, j: (i, 0), pipeline_mode=once),
                  pl.BlockSpec((tt, d), lambda i, j: (nxt(i), 0), pipeline_mode=once),
                  pl.BlockSpec((1, 6, d), lambda i, j: (mod_row(i), 0, 0)),
                  pl.BlockSpec((1, 6, d), lambda i, j: (mod_row(nxt(i)), 0, 0)),
                  pl.BlockSpec((1, d), lambda i, j: (0, 0)),
                  pl.BlockSpec((1, d), lambda i, j: (0, 0)),
                  pl.BlockSpec(wq_t.shape, lambda i, j: (0, 0), pipeline_mode=once),
                  pl.BlockSpec(sk.shape, lambda i, j: (0, 0, 0), pipeline_mode=once),
                  pl.BlockSpec((ec, d), lambda i, j: (j, 0)),
                  pl.BlockSpec((1, d, ec), lambda i, j: (j, 0, 0))],
        out_specs=pl.BlockSpec((tt, d), lambda i, j: (i, 0)),
        out_shape=jax.ShapeDtypeStruct((t, d), F32),
        scratch_shapes=scratch,
        compiler_params=_cparams(("arbitrary", "arbitrary")),
        name="peer_dense",
    )(x1, x1, mod, mod, norm2_g.reshape(1, d), final_g.reshape(1, d), wq_t, sk, u_bf, vt_bf)


def kernel(x_prompt, x_sample, cache_k, cache_v, c, c_ctx, w_mod, b_mod, norm1_g, w_in, lam_q1, lam_k1,
           lam_q2, lam_k2, subln_g, w_fourier, w_out, norm2_g, w_query, sub_keys, expert_u, expert_v,
           final_g):
    b, s, d = x_prompt.shape
    bd, sd, _ = x_sample.shape
    p = cache_k.shape[2]
    depth = w_mod.shape[0]
    assert depth == 1
    tm = 512
    tt, ec = 512, 1024

    rows = 16
    cond = jnp.zeros((rows, d), F32).at[0].set(c_ctx).at[1:1 + bd].set(c)
    tables = _rope_tables(sd)

    xp = x_prompt.reshape(b * s, d)
    xs = x_sample.reshape(bd * sd, d)
    l = 0
    lam_init = 0.8 - 0.6 * math.exp(-0.3 * l)
    mod = _modulation(cond, w_mod[l], b_mod[l]).reshape(rows, 6, d)
    w_in_bf = w_in[l].astype(BF16)
    w_out_bf = w_out[l].astype(BF16)
    w_f_bf = w_fourier[l].astype(BF16)
    wq_t = w_query[l].T.astype(BF16)
    sk = sub_keys[l].reshape(2 * PEER_HEADS, N_KEYS, -1).astype(BF16)
    u_bf = expert_u[l].astype(BF16)
    vt_bf = expert_v[l].reshape(-1, ec, d).transpose(0, 2, 1).astype(BF16)
    lams = (lam_q1[l], lam_k1[l], lam_q2[l], lam_k2[l])

    ctx_row = lambda i: 0
    lat_row = lambda i: 1 + i // (sd // tm)

    q, k, v, f = _inproj(xp, mod, norm1_g[l], w_in_bf, ctx_row, None, F32, tm)
    k3, v3 = k.reshape(b, s, ATT_W), v.reshape(b, s, ATT_W)
    att = _attention(q.reshape(b, s, ATT_W), (k3, v3), lams, subln_g[l], lam_init, s)
    four = _fourier(f.reshape(b, s, -1), w_f_bf)
    x1 = _outproj(xp, att.reshape(b * s, ATT_W), four.reshape(b * s, -1), w_out_bf, mod, ctx_row, tm)
    y_prompt = _peer(x1, mod, ctx_row, norm2_g[l], final_g, wq_t, sk, u_bf, vt_bf, tt, ec)
    state_k = k.reshape(b, 1, s, N_HEADS, 2 * HEAD_DIM)
    state_v = v.reshape(b, 1, s, N_HEADS, 2 * HEAD_DIM)

    q, k, v, f = _inproj(xs, mod, norm1_g[l], w_in_bf, lat_row, tables, BF16, tm)
    ck = cache_k[:, l].reshape(bd, p, ATT_W)
    cv = cache_v[:, l].reshape(bd, p, ATT_W)
    att = _attention(q.reshape(bd, sd, ATT_W),
                     (ck, cv, k.reshape(bd, sd, ATT_W), v.reshape(bd, sd, ATT_W)),
                     lams, subln_g[l], lam_init, 256)
    four = _fourier(f.reshape(bd, sd, -1), w_f_bf)
    x1 = _outproj(xs, att.reshape(bd * sd, ATT_W), four.reshape(bd * sd, -1), w_out_bf, mod, lat_row, tm)
    y_sample = _peer(x1, mod, lat_row, norm2_g[l], final_g, wq_t, sk, u_bf, vt_bf, tt, ec)

    return (y_prompt.reshape(b, s, d), y_sample.reshape(bd, sd, d), state_k, state_v)
```

```python
import functools
import math

import jax
import jax.numpy as jnp
import numpy as np
from jax import lax
from jax.experimental import pallas as pl
from jax.experimental.pallas import tpu as pltpu

F32 = jnp.float32
BF16 = jnp.bfloat16

EPS = 1e-6
LANES = 128
GRID_W = 64
N_HEADS = 4
HEAD_DIM = 64
ATT_W = N_HEADS * 2 * HEAD_DIM
N_GROUPS = 4
GROUP_DIM = 128
ROPE_BASE = 10000.0
PEER_HEADS = 8
N_KEYS = 128
TOPK = 16
ROWS = 16
GATE_BLOCKS = 8
GATE_ROWS = 128
VMEM_LIMIT = 56 * 1024 * 1024

_CAND_CNT = [TOPK // (k + 1) for k in range(TOPK)]


def _cparams(sem):
    return pltpu.CompilerParams(dimension_semantics=sem, vmem_limit_bytes=VMEM_LIMIT)


def _mod_kernel(c_ref, w_ref, b_ref, o_ref):
    c = c_ref[...]
    a = c / (1.0 + jnp.exp(-c))
    o_ref[...] = jnp.dot(a.astype(BF16), w_ref[...].astype(BF16), preferred_element_type=F32) + b_ref[...]


def _modulation(cond, w_mod, b_mod):
    rows, d = cond.shape
    n = w_mod.shape[1]
    tn = 1536
    return pl.pallas_call(
        _mod_kernel,
        grid=(n // tn,),
        in_specs=[pl.BlockSpec((rows, d), lambda j: (0, 0)),
                  pl.BlockSpec((d, tn), lambda j: (0, j)),
                  pl.BlockSpec((1, tn), lambda j: (0, j))],
        out_specs=pl.BlockSpec((rows, tn), lambda j: (0, j)),
        out_shape=jax.ShapeDtypeStruct((rows, n), F32),
        compiler_params=_cparams(("arbitrary",)),
        name="modulation",
    )(cond, w_mod, b_mod.reshape(1, n))


def _inproj_kernel(*refs, rope):
    if rope:
        x_ref, mod_ref, g_ref, w_ref, cos_ref, sa_ref, sb_ref, q_ref, k_ref, v_ref, f_ref = refs
    else:
        x_ref, mod_ref, g_ref, w_ref, q_ref, k_ref, v_ref, f_ref = refs
    x = x_ref[...]
    ms = jnp.mean(x * x, axis=-1, keepdims=True)
    xn = x * lax.rsqrt(ms + EPS) * g_ref[...]
    h = xn * (1.0 + mod_ref[0, 1:2, :]) + mod_ref[0, 0:1, :]
    z = jnp.dot(h.astype(BF16), w_ref[...], preferred_element_type=F32)
    scale = HEAD_DIM ** -0.5
    if rope:
        cos, sa, sb = cos_ref[...], sa_ref[...], sb_ref[...]
        for ch in range(ATT_W // LANES):
            lo, hi = ch * LANES, (ch + 1) * LANES
            for base, ref, sc in ((0, q_ref, scale), (ATT_W, k_ref, 1.0)):
                t = z[:, base + lo:base + hi]
                r = t * cos + pltpu.roll(t, LANES - 16, 1) * sa + pltpu.roll(t, 16, 1) * sb
                ref[:, lo:hi] = (r * sc).astype(ref.dtype)
    else:
        q_ref[...] = (z[:, :ATT_W] * scale).astype(q_ref.dtype)
        k_ref[...] = z[:, ATT_W:2 * ATT_W].astype(k_ref.dtype)
    v_ref[...] = z[:, 2 * ATT_W:3 * ATT_W].astype(v_ref.dtype)
    f_ref[...] = z[:, 3 * ATT_W:].astype(f_ref.dtype)


def _inproj(x2d, mod, norm_g, w_in_bf, mod_row, tables, kv_dtype, tm):
    t, d = x2d.shape
    n = w_in_bf.shape[1]
    fw = n - 3 * ATT_W
    rope = tables is not None
    in_specs = [pl.BlockSpec((tm, d), lambda i: (i, 0)),
                pl.BlockSpec((1, 6, d), lambda i: (mod_row(i), 0, 0)),
                pl.BlockSpec((1, d), lambda i: (0, 0)),
                pl.BlockSpec((d, n), lambda i: (0, 0))]
    args = [x2d, mod, norm_g.reshape(1, d), w_in_bf]
    if rope:
        seq = tables[0].shape[0]
        per = seq // tm
        for tb in tables:
            in_specs.append(pl.BlockSpec((tm, LANES), lambda i: (i % per, 0)))
            args.append(tb)
    return pl.pallas_call(
        functools.partial(_inproj_kernel, rope=rope),
        grid=(t // tm,),
        in_specs=in_specs,
        out_specs=[pl.BlockSpec((tm, ATT_W), lambda i: (i, 0)),
                   pl.BlockSpec((tm, ATT_W), lambda i: (i, 0)),
                   pl.BlockSpec((tm, ATT_W), lambda i: (i, 0)),
                   pl.BlockSpec((tm, fw), lambda i: (i, 0))],
        out_shape=[jax.ShapeDtypeStruct((t, ATT_W), BF16),
                   jax.ShapeDtypeStruct((t, ATT_W), kv_dtype),
                   jax.ShapeDtypeStruct((t, ATT_W), kv_dtype),
                   jax.ShapeDtypeStruct((t, fw), BF16)],
        compiler_params=_cparams(("arbitrary",)),
        name="inproj_rope" if rope else "inproj",
    )(*args)


def _rope_tables(seq):
    rows = seq // GRID_W
    row = np.repeat(np.arange(rows), GRID_W).astype(np.float32)
    col = np.tile(np.arange(GRID_W), rows).astype(np.float32)
    nf = HEAD_DIM // 4
    inv = jnp.asarray(ROPE_BASE, F32) ** (-jnp.arange(nf, dtype=F32) / nf)
    ar = jnp.asarray(row)[:, None] * inv
    ac = jnp.asarray(col)[:, None] * inv
    ang = jnp.concatenate([ar, ar, ac, ac], axis=-1)
    cos = jnp.tile(jnp.cos(ang), (1, LANES // HEAD_DIM))
    sin = jnp.tile(jnp.sin(ang), (1, LANES // HEAD_DIM))
    first = (np.arange(LANES) % 32) < 16
    sa = jnp.where(first[None, :], -sin, 0.0)
    sb = jnp.where(first[None, :], 0.0, sin)
    return cos, sa, sb


def _attn_kernel(*refs, n_pieces, lam_init):
    lq1, lk1, lq2, lk2, sg_ref, q_ref = refs[:6]
    kv = refs[6:6 + 2 * n_pieces]
    o_ref = refs[6 + 2 * n_pieces]
    l1 = jnp.sum(lq1[...] * lk1[...], axis=-1, keepdims=True)
    l2 = jnp.sum(lq2[...] * lk2[...], axis=-1, keepdims=True)
    lam = jnp.exp(l1) - jnp.exp(l2) + lam_init
    lane = lax.broadcasted_iota(jnp.int32, (1, LANES), 1)
    nt = (((1,), (1,)), ((), ()))
    for h in range(N_HEADS):
        hs = slice(h * LANES, (h + 1) * LANES)
        qh = q_ref[0, :, hs]
        s0, s1, vs = [], [], []
        for p in range(n_pieces):
            kp = kv[2 * p][0, :, hs].astype(BF16)
            vs.append(kv[2 * p + 1][0, :, hs].astype(BF16))
            k0 = jnp.where(lane < HEAD_DIM, kp, jnp.zeros_like(kp))
            k1 = jnp.where(lane >= HEAD_DIM, kp, jnp.zeros_like(kp))
            s0.append(lax.dot_general(qh, k0, nt, preferred_element_type=F32))
            s1.append(lax.dot_general(qh, k1, nt, preferred_element_type=F32))
        outs = []
        for ss in (s0, s1):
            m = ss[0].max(axis=-1, keepdims=True)
            for s in ss[1:]:
                m = jnp.maximum(m, s.max(axis=-1, keepdims=True))
            den, num = None, None
            for s, vp in zip(ss, vs):
                e = jnp.exp(s - m)
                d = e.sum(axis=-1, keepdims=True)
                t = jnp.dot(e.astype(BF16), vp, preferred_element_type=F32)
                den = d if den is None else den + d
                num = t if num is None else num + t
            outs.append(num * (1.0 / den))
        o = outs[0] - lam * outs[1]
        ms = jnp.mean(o * o, axis=-1, keepdims=True)
        a = o * lax.rsqrt(ms + EPS) * sg_ref[...] * (1.0 - lam_init)
        o_ref[0, :, hs] = a.astype(o_ref.dtype)


def _attention(q, kvs, lams, subln_g, lam_init, tq):
    b, sq, w = q.shape
    n_pieces = len(kvs) // 2
    lam_specs = [pl.BlockSpec((1, HEAD_DIM), lambda i, j: (0, 0)) for _ in range(4)]
    in_specs = lam_specs + [pl.BlockSpec((1, LANES), lambda i, j: (0, 0)),
                            pl.BlockSpec((1, tq, w), lambda i, j: (i, j, 0))]
    for a in kvs:
        in_specs.append(pl.BlockSpec((1, a.shape[1], w), lambda i, j: (i, 0, 0)))
    return pl.pallas_call(
        functools.partial(_attn_kernel, n_pieces=n_pieces, lam_init=lam_init),
        grid=(b, sq // tq),
        in_specs=in_specs,
        out_specs=pl.BlockSpec((1, tq, w), lambda i, j: (i, j, 0)),
        out_shape=jax.ShapeDtypeStruct((b, sq, w), BF16),
        compiler_params=_cparams(("arbitrary", "arbitrary")),
        name="diff_attention_%d" % n_pieces,
    )(*[l.reshape(1, HEAD_DIM) for l in lams], subln_g.reshape(1, LANES), q, *kvs)


def _fourier_kernel(f_ref, wcs_ref, m2_ref, wf_ref, o_ref):
    gslices = [slice(g * GROUP_DIM, (g + 1) * GROUP_DIM) for g in range(N_GROUPS)]
    ys = [jnp.dot(f_ref[0, :, gs], wcs_ref[...], preferred_element_type=F32) for gs in gslices]
    yc = jnp.concatenate([jnp.concatenate([y[:, :GROUP_DIM] for y in ys], axis=1),
                          jnp.concatenate([y[:, GROUP_DIM:] for y in ys], axis=1)], axis=0).astype(BF16)
    spec = jnp.dot(m2_ref[...], yc, preferred_element_type=F32).astype(BF16)
    for g, gs in enumerate(gslices):
        o_ref[0, :, gs] = jnp.dot(spec[:, gs], wf_ref[g], preferred_element_type=F32).astype(o_ref.dtype)


def _dft_tables(seq):
    kc = np.arange(GROUP_DIM)
    ang_c = 2.0 * np.pi * ((kc[:, None] * kc[None, :]) % GROUP_DIM) / GROUP_DIM
    wcs = np.concatenate([np.cos(ang_c), np.sin(ang_c)], axis=1) / math.sqrt(GROUP_DIM)
    ks = np.arange(seq)
    ang_s = 2.0 * np.pi * ((ks[:, None] * ks[None, :]) % seq) / seq
    m2 = np.concatenate([np.cos(ang_s), -np.sin(ang_s)], axis=1) / math.sqrt(seq)
    return jnp.asarray(wcs, F32).astype(BF16), jnp.asarray(m2, F32).astype(BF16)


def _fourier(f, w_f_bf):
    b, s, w = f.shape
    wcs, m2 = _dft_tables(s)
    return pl.pallas_call(
        _fourier_kernel,
        grid=(b,),
        in_specs=[pl.BlockSpec((1, s, w), lambda i: (i, 0, 0)),
                  pl.BlockSpec(wcs.shape, lambda i: (0, 0)),
                  pl.BlockSpec(m2.shape, lambda i: (0, 0)),
                  pl.BlockSpec(w_f_bf.shape, lambda i: (0, 0, 0))],
        out_specs=pl.BlockSpec((1, s, w), lambda i: (i, 0, 0)),
        out_shape=jax.ShapeDtypeStruct((b, s, w), BF16),
        compiler_params=_cparams(("arbitrary",)),
        name="fourier_mix",
    )(f, wcs, m2, w_f_bf)


def _outproj_kernel(x_ref, a_ref, f_ref, wa_ref, wf_ref, mod_ref, o_ref):
    m = jnp.dot(a_ref[...], wa_ref[...], preferred_element_type=F32)
    m = m + jnp.dot(f_ref[...], wf_ref[...], preferred_element_type=F32)
    o_ref[...] = x_ref[...] + mod_ref[0, 2:3, :] * m


def _outproj(x2d, a2d, f2d, w_out_bf, mod, mod_row, tm):
    t, d = x2d.shape
    wa, wf = w_out_bf[:ATT_W], w_out_bf[ATT_W:]
    return pl.pallas_call(
        _outproj_kernel,
        grid=(t // tm,),
        in_specs=[pl.BlockSpec((tm, d), lambda i: (i, 0)),
                  pl.BlockSpec((tm, ATT_W), lambda i: (i, 0)),
                  pl.BlockSpec((tm, wf.shape[0]), lambda i: (i, 0)),
                  pl.BlockSpec(wa.shape, lambda i: (0, 0)),
                  pl.BlockSpec(wf.shape, lambda i: (0, 0)),
                  pl.BlockSpec((1, 6, d), lambda i: (mod_row(i), 0, 0))],
        out_specs=pl.BlockSpec((tm, d), lambda i: (i, 0)),
        out_shape=jax.ShapeDtypeStruct((t, d), F32),
        compiler_params=_cparams(("arbitrary",)),
        name="outproj",
    )(x2d, a2d, f2d, wa, wf, mod)


def _top16(s, ids, exact, want_rank=True):
    work = s
    rank = jnp.full(s.shape, float(TOPK), F32) if want_rank else None
    vals = []
    for r in range(TOPK):
        m = jnp.max(work, axis=0, keepdims=True)
        sel = work == m
        if exact:
            sel = ids == jnp.min(jnp.where(sel, ids, 1e9), axis=0, keepdims=True)
        if want_rank:
            rank = jnp.where(sel, float(r), rank)
        work = jnp.where(sel, -jnp.inf, work)
        vals.append(m)
    return rank, vals


def _rows_to_array(rows):
    it = lax.broadcasted_iota(jnp.int32, (TOPK, LANES), 0)
    out = jnp.zeros((TOPK, LANES), F32)
    for r, row in enumerate(rows):
        out = jnp.where(it == r, row, out)
    return out


def _candidates(v1a, v2a, exact):
    it8 = lax.broadcasted_iota(jnp.int32, (8, LANES), 0)
    it16 = lax.broadcasted_iota(jnp.int32, (TOPK, LANES), 0)
    blocks = [v1a[0:1, :] + v2a]
    idb = [it16.astype(F32)]
    for k1 in range(1, 8):
        blocks.append(jnp.where(it8 < _CAND_CNT[k1], v1a[k1:k1 + 1, :] + v2a[0:8, :], -jnp.inf))
        idb.append((it8 + k1 * TOPK).astype(F32))
    blocks.append(v1a[8:16, :] + v2a[0:1, :])
    idb.append(((it8 + 8) * TOPK).astype(F32))
    cand = jnp.concatenate(blocks, axis=0)
    cid = jnp.concatenate(idb, axis=0)
    rc, top = _top16(cand, cid, exact)
    z = jnp.zeros((1, LANES), F32)
    for r in range(TOPK):
        z = z + jnp.exp(top[r] - top[0])
    picked = jnp.where(rc < float(TOPK), 1.0, 0.0)
    cnt_lo = jnp.zeros((8, LANES), F32)
    for k1 in range(8):
        lo = 0 if k1 == 0 else 8 + 8 * k1
        n = jnp.sum(picked[lo:lo + (TOPK if k1 == 0 else 8), :], axis=0, keepdims=True)
        cnt_lo = jnp.where(it8 == k1, n, cnt_lo)
    cnt = jnp.concatenate([cnt_lo, picked[72:80, :]], axis=0)
    return cnt, z


def _count_ne16(mask_f32):
    return jnp.where(jnp.sum(mask_f32, axis=0, keepdims=True) != float(TOPK), 1.0, 0.0)


def _peer_gates_exact(s1, s2):
    ids = lax.broadcasted_iota(jnp.int32, (N_KEYS, LANES), 0).astype(F32)
    r1, v1 = _top16(s1, ids, True)
    r2, v2 = _top16(s2, ids, True)
    cnt, z = _candidates(_rows_to_array(v1), _rows_to_array(v2), True)
    c1 = jnp.zeros((N_KEYS, LANES), F32)
    for k in range(TOPK):
        c1 = jnp.where(r1 == float(k), cnt[k:k + 1, :], c1)
    w1 = jnp.exp(s1 - v1[0]) * (1.0 / z)
    e2 = jnp.exp(s2 - v2[0])
    return r2, e2, c1, w1


def _peer_kernel(x_ref, xn_ref, mod_ref, modn_ref, g2_ref, gf_ref, wq_ref, sk_ref, u_ref, vt_ref, y_ref,
                 h2t_s, s_s, v_s, r2_s, e2_s, c1_s, w1_s, at_s, act_s, acc_s, redo_s, *, tt, ec):
    i, j = pl.program_id(0), pl.program_id(1)
    nlg = tt // LANES
    nb = ec // N_KEYS
    n_items = PEER_HEADS * nlg
    cur = i % 2
    nxt = 1 - cur
    want = float(TOPK)

    def project(xr, modr, par):
        x = xr[...]
        ms = jnp.mean(x * x, axis=-1, keepdims=True)
        xn = x * lax.rsqrt(ms + EPS) * g2_ref[...]
        h2 = xn * (1.0 + modr[0, 4:5, :]) + modr[0, 3:4, :]
        h2t_s[par] = h2.T.astype(BF16)
        per = ec // N_KEYS
        for part in range(2 * PEER_HEADS // per):
            act_s[:, 0:tt] = jnp.dot(wq_ref[part * ec:(part + 1) * ec, :], h2t_s[par],
                                     preferred_element_type=F32).astype(BF16)
            for k in range(per):
                s_s[part * per + k] = jnp.dot(sk_ref[part * per + k], act_s[k * N_KEYS:(k + 1) * N_KEYS, 0:tt],
                                              preferred_element_type=F32)

    def scores(idx):
        h, lg = idx // nlg, idx % nlg
        c0 = pl.multiple_of(lg * LANES, LANES)
        return h, lg, s_s[2 * h, :, pl.ds(c0, LANES)], s_s[2 * h + 1, :, pl.ds(c0, LANES)]

    def store_i2(par, h, lg, r2, e2):
        r2_s[par * PEER_HEADS + h, lg] = r2.astype(BF16)
        e2_s[par * PEER_HEADS + h, lg] = e2.astype(BF16)

    def store_i1(par, h, lg, c1, w1):
        c1_s[par * nlg + lg, pl.ds(h, N_KEYS, stride=PEER_HEADS), :] = c1
        w1_s[par * nlg + lg, pl.ds(h, N_KEYS, stride=PEER_HEADS), :] = w1

    def flag(row):
        return (jnp.max(row) > 0.0).astype(jnp.int32)

    def sides(idx, par, dep):
        h, lg, s1, s2 = scores(idx)
        _, v1 = _top16(s1 + dep, None, False, want_rank=False)
        r2, v2 = _top16(s2, None, False)
        store_i2(par, h, lg, r2, jnp.exp(s2 - v2[0]))
        v_s[idx, 0:TOPK, :] = _rows_to_array(v1)
        v_s[idx, TOPK:2 * TOPK, :] = _rows_to_array(v2)
        return (_count_ne16(jnp.where(s1 >= v1[TOPK - 1], 1.0, 0.0))
                + _count_ne16(jnp.where(r2 < want, 1.0, 0.0)))

    def cells(idx, par):
        h, lg, s1, _ = scores(idx)
        v1a = v_s[idx, 0:TOPK, :]
        cnt, z = _candidates(v1a, v_s[idx, TOPK:2 * TOPK, :], False)
        c1 = jnp.zeros((N_KEYS, LANES), F32)
        for k in range(TOPK):
            c1 = jnp.where(s1 == v1a[k:k + 1, :], cnt[k:k + 1, :], c1)
        store_i1(par, h, lg, c1, jnp.exp(s1 - v1a[0:1, :]) * (1.0 / z))
        redo_s[idx + 1] = redo_s[idx + 1] + flag(_count_ne16(cnt))

    @pl.when(j == 0)
    def _prologue():
        @pl.when(i == 0)
        def _first_tile():
            project(x_ref, mod_ref, 0)

            def side_trip(idx, prev_bad):
                redo_s[idx] = flag(prev_bad)
                return sides(idx, 0, jnp.zeros((1, LANES), F32))
            redo_s[n_items] = flag(lax.fori_loop(0, n_items, side_trip, jnp.zeros((1, LANES), F32)))

            def cell_trip(pair, carry):
                cells(2 * pair, 0)
                cells(2 * pair + 1, 0)
                return carry
            lax.fori_loop(0, n_items // 2, cell_trip, 0)

        def redo(idx, carry):
            @pl.when(redo_s[idx + 1] > 0)
            def _ties():
                h, lg, s1, s2 = scores(idx)
                r2, e2, c1, w1 = _peer_gates_exact(s1, s2)
                store_i2(cur, h, lg, r2, e2)
                store_i1(cur, h, lg, c1, w1)
            return carry
        lax.fori_loop(0, n_items, redo, 0)

        project(xn_ref, modn_ref, nxt)
        acc_s[...] = jnp.zeros_like(acc_s)

    at_s[:, 0:tt] = jnp.dot(u_ref[...], h2t_s[cur], preferred_element_type=F32)
    bad = sides(2 * j, nxt, jnp.zeros((1, LANES), F32))
    redo_s[2 * j + 1] = flag(bad)
    redo_s[2 * j + 2] = flag(sides(2 * j + 1, nxt, jnp.minimum(bad, 0.0)))

    n_sub = N_KEYS // GATE_ROWS
    n_bg = nb // GATE_BLOCKS

    def gate_trip(it, carry):
        l = it // (n_sub * n_bg)
        row0 = (it // n_bg % n_sub) * GATE_ROWS
        b0 = (it % n_bg) * GATE_BLOCKS
        c0 = pl.multiple_of(l * LANES, LANES)
        cb, wb = [], []
        for b in range(GATE_BLOCKS):
            i1 = pl.multiple_of((j * nb + b0 + b) * PEER_HEADS, PEER_HEADS)
            cb.append(c1_s[cur * nlg + l, pl.ds(i1, PEER_HEADS), :].astype(BF16))
            wb.append(w1_s[cur * nlg + l, pl.ds(i1, PEER_HEADS), :].astype(BF16))
        subs = range(GATE_ROWS // ROWS)
        g = [[jnp.zeros((ROWS, LANES), BF16) for _ in range(GATE_BLOCKS)] for _ in subs]
        zero = jnp.zeros((), BF16)
        for h in range(PEER_HEADS):
            for s in subs:
                rows = pl.ds(pl.multiple_of(row0 + s * ROWS, ROWS), ROWS)
                r2t = r2_s[cur * PEER_HEADS + h, l, rows, :]
                e2t = e2_s[cur * PEER_HEADS + h, l, rows, :]
                for b in range(GATE_BLOCKS):
                    g[s][b] = g[s][b] + e2t * jnp.where(r2t < cb[b][h:h + 1, :], wb[b][h:h + 1, :], zero)
        for s in subs:
            for b in range(GATE_BLOCKS):
                rows = pl.ds(pl.multiple_of((b0 + b) * N_KEYS + row0 + s * ROWS, ROWS), ROWS)
                a = at_s[rows, pl.ds(c0, LANES)]
                gelu = 0.5 * a * (1.0 + lax.erf(a * (2.0 ** -0.5)))
                act_s[rows, pl.ds(c0, LANES)] = g[s][b] * gelu.astype(BF16)
        return carry
    lax.fori_loop(0, nlg * n_sub * n_bg, gate_trip, 0)

    acc_s[:, 0:tt] += jnp.dot(vt_ref[0], act_s[:, 0:tt], preferred_element_type=F32)
    cells(2 * j, nxt)
    cells(2 * j + 1, nxt)

    @pl.when(j == pl.num_programs(1) - 1)
    def _epilogue():
        x2 = x_ref[...] + mod_ref[0, 5:6, :] * acc_s[:, 0:tt].T
        ms = jnp.mean(x2 * x2, axis=-1, keepdims=True)
        y_ref[...] = x2 * lax.rsqrt(ms + EPS) * gf_ref[...]


def _peer(x1, mod, mod_row, norm2_g, final_g, wq_t, sk, u_bf, vt_bf, tt, ec):
    t, d = x1.shape
    ne = u_bf.shape[0]
    nt, nc, nlg = t // tt, ne // ec, tt // LANES
    assert PEER_HEADS * nlg == 2 * nc
    scratch = [pltpu.VMEM((2, d, tt), BF16),
               pltpu.VMEM((2 * PEER_HEADS, N_KEYS, tt), F32),
               pltpu.VMEM((PEER_HEADS * nlg, 2 * TOPK, LANES), F32),
               pltpu.VMEM((2 * PEER_HEADS, nlg, N_KEYS, LANES), BF16),
               pltpu.VMEM((2 * PEER_HEADS, nlg, N_KEYS, LANES), BF16),
               pltpu.VMEM((2 * nlg, N_KEYS * PEER_HEADS, LANES), F32),
               pltpu.VMEM((2 * nlg, N_KEYS * PEER_HEADS, LANES), F32),
               pltpu.VMEM((ec, tt + LANES), F32),
               pltpu.VMEM((ec, tt + LANES), BF16),
               pltpu.VMEM((d, tt + LANES), F32),
               pltpu.SMEM((PEER_HEADS * nlg + 1,), jnp.int32)]
    once = pl.Buffered(1)
    nxt = lambda i: jnp.minimum(i + 1, nt - 1)
    return pl.pallas_call(
        functools.partial(_peer_kernel, tt=tt, ec=ec),
        grid=(nt, nc),
        in_specs=[pl.BlockSpec((tt, d), lambda i, j: (i, 0), pipeline_mode=once),
                  pl.BlockSpec((tt, d), lambda i, j: (nxt(i), 0), pipeline_mode=once),
                  pl.BlockSpec((1, 6, d), lambda i, j: (mod_row(i), 0, 0)),
                  pl.BlockSpec((1, 6, d), lambda i, j: (mod_row(nxt(i)), 0, 0)),
                  pl.BlockSpec((1, d), lambda i, j: (0, 0)),
                  pl.BlockSpec((1, d), lambda i, j: (0, 0)),
                  pl.BlockSpec(wq_t.shape, lambda i, j: (0, 0), pipeline_mode=once),
                  pl.BlockSpec(sk.shape, lambda i, j: (0, 0, 0), pipeline_mode=once),
                  pl.BlockSpec((ec, d), lambda i, j: (j, 0)),
                  pl.BlockSpec((1, d, ec), lambda i, j: (j, 0, 0))],
        out_specs=pl.BlockSpec((tt, d), lambda i, j: (i, 0)),
        out_shape=jax.ShapeDtypeStruct((t, d), F32),
        scratch_shapes=scratch,
        compiler_params=_cparams(("arbitrary", "arbitrary")),
        name="peer_dense",
    )(x1, x1, mod, mod, norm2_g.reshape(1, d), final_g.reshape(1, d), wq_t, sk, u_bf, vt_bf)


def kernel(x_prompt, x_sample, cache_k, cache_v, c, c_ctx, w_mod, b_mod, norm1_g, w_in, lam_q1, lam_k1,
           lam_q2, lam_k2, subln_g, w_fourier, w_out, norm2_g, w_query, sub_keys, expert_u, expert_v,
           final_g):
    b, s, d = x_prompt.shape
    bd, sd, _ = x_sample.shape
    p = cache_k.shape[2]
    depth = w_mod.shape[0]
    assert depth == 1
    tm = 512
    tt, ec = 512, 1024

    rows = 16
    cond = jnp.zeros((rows, d), F32).at[0].set(c_ctx).at[1:1 + bd].set(c)
    tables = _rope_tables(sd)

    xp = x_prompt.reshape(b * s, d)
    xs = x_sample.reshape(bd * sd, d)
    l = 0
    lam_init = 0.8 - 0.6 * math.exp(-0.3 * l)
    mod = _modulation(cond, w_mod[l], b_mod[l]).reshape(rows, 6, d)
    w_in_bf = w_in[l].astype(BF16)
    w_out_bf = w_out[l].astype(BF16)
    w_f_bf = w_fourier[l].astype(BF16)
    wq_t = w_query[l].T.astype(BF16)
    sk = sub_keys[l].reshape(2 * PEER_HEADS, N_KEYS, -1).astype(BF16)
    u_bf = expert_u[l].astype(BF16)
    vt_bf = expert_v[l].reshape(-1, ec, d).transpose(0, 2, 1).astype(BF16)
    lams = (lam_q1[l], lam_k1[l], lam_q2[l], lam_k2[l])

    ctx_row = lambda i: 0
    lat_row = lambda i: 1 + i // (sd // tm)

    q, k, v, f = _inproj(xp, mod, norm1_g[l], w_in_bf, ctx_row, None, F32, tm)
    k3, v3 = k.reshape(b, s, ATT_W), v.reshape(b, s, ATT_W)
    att = _attention(q.reshape(b, s, ATT_W), (k3, v3), lams, subln_g[l], lam_init, s)
    four = _fourier(f.reshape(b, s, -1), w_f_bf)
    x1 = _outproj(xp, att.reshape(b * s, ATT_W), four.reshape(b * s, -1), w_out_bf, mod, ctx_row, tm)
    y_prompt = _peer(x1, mod, ctx_row, norm2_g[l], final_g, wq_t, sk, u_bf, vt_bf, tt, ec)
    state_k = k.reshape(b, 1, s, N_HEADS, 2 * HEAD_DIM)
    state_v = v.reshape(b, 1, s, N_HEADS, 2 * HEAD_DIM)

    q, k, v, f = _inproj(xs, mod, norm1_g[l], w_in_bf, lat_row, tables, BF16, tm)
    ck = cache_k[:, l].reshape(bd, p, ATT_W)
    cv = cache_v[:, l].reshape(bd, p, ATT_W)
    att = _attention(q.reshape(bd, sd, ATT_W),
                     (ck, cv, k.reshape(bd, sd, ATT_W), v.reshape(bd, sd, ATT_W)),
                     lams, subln_g[l], lam_init, 512)
    four = _fourier(f.reshape(bd, sd, -1), w_f_bf)
    x1 = _outproj(xs, att.reshape(bd * sd, ATT_W), four.reshape(bd * sd, -1), w_out_bf, mod, lat_row, tm)
    y_sample = _peer(x1, mod, lat_row, norm2_g[l], final_g, wq_t, sk, u_bf, vt_bf, tt, ec)

    return (y_prompt.reshape(b, s, d), y_sample.reshape(bd, sd, d), state_k, state_v)
```

```python
import functools
import math

import jax
import jax.numpy as jnp
import numpy as np
from jax import lax
from jax.experimental import pallas as pl
from jax.experimental.pallas import tpu as pltpu

F32 = jnp.float32
BF16 = jnp.bfloat16

EPS = 1e-6
LANES = 128
GRID_W = 64
N_HEADS = 4
HEAD_DIM = 64
ATT_W = N_HEADS * 2 * HEAD_DIM
N_GROUPS = 4
GROUP_DIM = 128
ROPE_BASE = 10000.0
PEER_HEADS = 8
N_KEYS = 128
TOPK = 16
ROWS = 16
GATE_BLOCKS = 8
GATE_ROWS = 128
VMEM_LIMIT = 56 * 1024 * 1024

_CAND_CNT = [TOPK // (k + 1) for k in range(TOPK)]


def _cparams(sem):
    return pltpu.CompilerParams(dimension_semantics=sem, vmem_limit_bytes=VMEM_LIMIT)


def _mod_kernel(c_ref, w_ref, b_ref, o_ref):
    c = c_ref[...]
    a = c / (1.0 + jnp.exp(-c))
    o_ref[...] = jnp.dot(a.astype(BF16), w_ref[...].astype(BF16), preferred_element_type=F32) + b_ref[...]


def _modulation(cond, w_mod, b_mod):
    rows, d = cond.shape
    n = w_mod.shape[1]
    tn = 1536
    return pl.pallas_call(
        _mod_kernel,
        grid=(n // tn,),
        in_specs=[pl.BlockSpec((rows, d), lambda j: (0, 0)),
                  pl.BlockSpec((d, tn), lambda j: (0, j)),
                  pl.BlockSpec((1, tn), lambda j: (0, j))],
        out_specs=pl.BlockSpec((rows, tn), lambda j: (0, j)),
        out_shape=jax.ShapeDtypeStruct((rows, n), F32),
        compiler_params=_cparams(("arbitrary",)),
        name="modulation",
    )(cond, w_mod, b_mod.reshape(1, n))


def _inproj_kernel(*refs, rope):
    if rope:
        x_ref, mod_ref, g_ref, w_ref, cos_ref, sa_ref, sb_ref, q_ref, k_ref, v_ref, f_ref = refs
    else:
        x_ref, mod_ref, g_ref, w_ref, q_ref, k_ref, v_ref, f_ref = refs
    x = x_ref[...]
    ms = jnp.mean(x * x, axis=-1, keepdims=True)
    xn = x * lax.rsqrt(ms + EPS) * g_ref[...]
    h = xn * (1.0 + mod_ref[0, 1:2, :]) + mod_ref[0, 0:1, :]
    z = jnp.dot(h.astype(BF16), w_ref[...], preferred_element_type=F32)
    scale = HEAD_DIM ** -0.5
    if rope:
        cos, sa, sb = cos_ref[...], sa_ref[...], sb_ref[...]
        for ch in range(ATT_W // LANES):
            lo, hi = ch * LANES, (ch + 1) * LANES
            for base, ref, sc in ((0, q_ref, scale), (ATT_W, k_ref, 1.0)):
                t = z[:, base + lo:base + hi]
                r = t * cos + pltpu.roll(t, LANES - 16, 1) * sa + pltpu.roll(t, 16, 1) * sb
                ref[:, lo:hi] = (r * sc).astype(ref.dtype)
    else:
        q_ref[...] = (z[:, :ATT_W] * scale).astype(q_ref.dtype)
        k_ref[...] = z[:, ATT_W:2 * ATT_W].astype(k_ref.dtype)
    v_ref[...] = z[:, 2 * ATT_W:3 * ATT_W].astype(v_ref.dtype)
    f_ref[...] = z[:, 3 * ATT_W:].astype(f_ref.dtype)


def _inproj(x2d, mod, norm_g, w_in_bf, mod_row, tables, kv_dtype, tm):
    t, d = x2d.shape
    n = w_in_bf.shape[1]
    fw = n - 3 * ATT_W
    rope = tables is not None
    in_specs = [pl.BlockSpec((tm, d), lambda i: (i, 0)),
                pl.BlockSpec((1, 6, d), lambda i: (mod_row(i), 0, 0)),
                pl.BlockSpec((1, d), lambda i: (0, 0)),
                pl.BlockSpec((d, n), lambda i: (0, 0))]
    args = [x2d, mod, norm_g.reshape(1, d), w_in_bf]
    if rope:
        seq = tables[0].shape[0]
        per = seq // tm
        for tb in tables:
            in_specs.append(pl.BlockSpec((tm, LANES), lambda i: (i % per, 0)))
            args.append(tb)
    return pl.pallas_call(
        functools.partial(_inproj_kernel, rope=rope),
        grid=(t // tm,),
        in_specs=in_specs,
        out_specs=[pl.BlockSpec((tm, ATT_W), lambda i: (i, 0)),
                   pl.BlockSpec((tm, ATT_W), lambda i: (i, 0)),
                   pl.BlockSpec((tm, ATT_W), lambda i: (i, 0)),
                   pl.BlockSpec((tm, fw), lambda i: (i, 0))],
        out_shape=[jax.ShapeDtypeStruct((t, ATT_W), BF16),
                   jax.ShapeDtypeStruct((t, ATT_W), kv_dtype),
                   jax.ShapeDtypeStruct((t, ATT_W), kv_dtype),
                   jax.ShapeDtypeStruct((t, fw), BF16)],
        compiler_params=_cparams(("arbitrary",)),
        name="inproj_rope" if rope else "inproj",
    )(*args)


def _rope_tables(seq):
    rows = seq // GRID_W
    row = np.repeat(np.arange(rows), GRID_W).astype(np.float32)
    col = np.tile(np.arange(GRID_W), rows).astype(np.float32)
    nf = HEAD_DIM // 4
    inv = jnp.asarray(ROPE_BASE, F32) ** (-jnp.arange(nf, dtype=F32) / nf)
    ar = jnp.asarray(row)[:, None] * inv
    ac = jnp.asarray(col)[:, None] * inv
    ang = jnp.concatenate([ar, ar, ac, ac], axis=-1)
    cos = jnp.tile(jnp.cos(ang), (1, LANES // HEAD_DIM))
    sin = jnp.tile(jnp.sin(ang), (1, LANES // HEAD_DIM))
    first = (np.arange(LANES) % 32) < 16
    sa = jnp.where(first[None, :], -sin, 0.0)
    sb = jnp.where(first[None, :], 0.0, sin)
    return cos, sa, sb


def _attn_kernel(*refs, n_pieces, lam_init):
    lq1, lk1, lq2, lk2, sg_ref, q_ref = refs[:6]
    kv = refs[6:6 + 2 * n_pieces]
    o_ref = refs[6 + 2 * n_pieces]
    l1 = jnp.sum(lq1[...] * lk1[...], axis=-1, keepdims=True)
    l2 = jnp.sum(lq2[...] * lk2[...], axis=-1, keepdims=True)
    lam = jnp.exp(l1) - jnp.exp(l2) + lam_init
    lane = lax.broadcasted_iota(jnp.int32, (1, LANES), 1)
    nt = (((1,), (1,)), ((), ()))
    for h in range(N_HEADS):
        hs = slice(h * LANES, (h + 1) * LANES)
        qh = q_ref[0, :, hs]
        s0, s1, vs = [], [], []
        for p in range(n_pieces):
            kp = kv[2 * p][0, :, hs].astype(BF16)
            vs.append(kv[2 * p + 1][0, :, hs].astype(BF16))
            k0 = jnp.where(lane < HEAD_DIM, kp, jnp.zeros_like(kp))
            k1 = jnp.where(lane >= HEAD_DIM, kp, jnp.zeros_like(kp))
            s0.append(lax.dot_general(qh, k0, nt, preferred_element_type=F32))
            s1.append(lax.dot_general(qh, k1, nt, preferred_element_type=F32))
        outs = []
        for ss in (s0, s1):
            m = ss[0].max(axis=-1, keepdims=True)
            for s in ss[1:]:
                m = jnp.maximum(m, s.max(axis=-1, keepdims=True))
            den, num = None, None
            for s, vp in zip(ss, vs):
                e = jnp.exp(s - m)
                d = e.sum(axis=-1, keepdims=True)
                t = jnp.dot(e.astype(BF16), vp, preferred_element_type=F32)
                den = d if den is None else den + d
                num = t if num is None else num + t
            outs.append(num * (1.0 / den))
        o = outs[0] - lam * outs[1]
        ms = jnp.mean(o * o, axis=-1, keepdims=True)
        a = o * lax.rsqrt(ms + EPS) * sg_ref[...] * (1.0 - lam_init)
        o_ref[0, :, hs] = a.astype(o_ref.dtype)


def _attention(q, kvs, lams, subln_g, lam_init, tq):
    b, sq, w = q.shape
    n_pieces = len(kvs) // 2
    lam_specs = [pl.BlockSpec((1, HEAD_DIM), lambda i, j: (0, 0)) for _ in range(4)]
    in_specs = lam_specs + [pl.BlockSpec((1, LANES), lambda i, j: (0, 0)),
                            pl.BlockSpec((1, tq, w), lambda i, j: (i, j, 0))]
    for a in kvs:
        in_specs.append(pl.BlockSpec((1, a.shape[1], w), lambda i, j: (i, 0, 0)))
    return pl.pallas_call(
        functools.partial(_attn_kernel, n_pieces=n_pieces, lam_init=lam_init),
        grid=(b, sq // tq),
        in_specs=in_specs,
        out_specs=pl.BlockSpec((1, tq, w), lambda i, j: (i, j, 0)),
        out_shape=jax.ShapeDtypeStruct((b, sq, w), BF16),
        compiler_params=_cparams(("arbitrary", "arbitrary")),
        name="diff_attention_%d" % n_pieces,
    )(*[l.reshape(1, HEAD_DIM) for l in lams], subln_g.reshape(1, LANES), q, *kvs)


def _fourier_kernel(f_ref, wcs_ref, m2_ref, wf_ref, o_ref):
    gslices = [slice(g * GROUP_DIM, (g + 1) * GROUP_DIM) for g in range(N_GROUPS)]
    ys = [jnp.dot(f_ref[0, :, gs], wcs_ref[...], preferred_element_type=F32) for gs in gslices]
    yc = jnp.concatenate([jnp.concatenate([y[:, :GROUP_DIM] for y in ys], axis=1),
                          jnp.concatenate([y[:, GROUP_DIM:] for y in ys], axis=1)], axis=0).astype(BF16)
    spec = jnp.dot(m2_ref[...], yc, preferred_element_type=F32).astype(BF16)
    for g, gs in enumerate(gslices):
        o_ref[0, :, gs] = jnp.dot(spec[:, gs], wf_ref[g], preferred_element_type=F32).astype(o_ref.dtype)


def _dft_tables(seq):
    kc = np.arange(GROUP_DIM)
    ang_c = 2.0 * np.pi * ((kc[:, None] * kc[None, :]) % GROUP_DIM) / GROUP_DIM
    wcs = np.concatenate([np.cos(ang_c), np.sin(ang_c)], axis=1) / math.sqrt(GROUP_DIM)
    ks = np.arange(seq)
    ang_s = 2.0 * np.pi * ((ks[:, None] * ks[None, :]) % seq) / seq
    m2 = np.concatenate([np.cos(ang_s), -np.sin(ang_s)], axis=1) / math.sqrt(seq)
    return jnp.asarray(wcs, F32).astype(BF16), jnp.asarray(m2, F32).astype(BF16)


def _fourier(f, w_f_bf):
    b, s, w = f.shape
    wcs, m2 = _dft_tables(s)
    return pl.pallas_call(
        _fourier_kernel,
        grid=(b,),
        in_specs=[pl.BlockSpec((1, s, w), lambda i: (i, 0, 0)),
                  pl.BlockSpec(wcs.shape, lambda i: (0, 0)),
                  pl.BlockSpec(m2.shape, lambda i: (0, 0)),
                  pl.BlockSpec(w_f_bf.shape, lambda i: (0, 0, 0))],
        out_specs=pl.BlockSpec((1, s, w), lambda i: (i, 0, 0)),
        out_shape=jax.ShapeDtypeStruct((b, s, w), BF16),
        compiler_params=_cparams(("arbitrary",)),
        name="fourier_mix",
    )(f, wcs, m2, w_f_bf)


def _outproj_kernel(x_ref, a_ref, f_ref, wa_ref, wf_ref, mod_ref, o_ref):
    m = jnp.dot(a_ref[...], wa_ref[...], preferred_element_type=F32)
    m = m + jnp.dot(f_ref[...], wf_ref[...], preferred_element_type=F32)
    o_ref[...] = x_ref[...] + mod_ref[0, 2:3, :] * m


def _outproj(x2d, a2d, f2d, w_out_bf, mod, mod_row, tm):
    t, d = x2d.shape
    wa, wf = w_out_bf[:ATT_W], w_out_bf[ATT_W:]
    return pl.pallas_call(
        _outproj_kernel,
        grid=(t // tm,),
        in_specs=[pl.BlockSpec((tm, d), lambda i: (i, 0)),
                  pl.BlockSpec((tm, ATT_W), lambda i: (i, 0)),
                  pl.BlockSpec((tm, wf.shape[0]), lambda i: (i, 0)),
                  pl.BlockSpec(wa.shape, lambda i: (0, 0)),
                  pl.BlockSpec(wf.shape, lambda i: (0, 0)),
                  pl.BlockSpec((1, 6, d), lambda i: (mod_row(i), 0, 0))],
        out_specs=pl.BlockSpec((tm, d), lambda i: (i, 0)),
        out_shape=jax.ShapeDtypeStruct((t, d), F32),
        compiler_params=_cparams(("arbitrary",)),
        name="outproj",
    )(x2d, a2d, f2d, wa, wf, mod)


def _top16(s, ids, exact, want_rank=True):
    work = s
    rank = jnp.full(s.shape, float(TOPK), F32) if want_rank else None
    vals = []
    for r in range(TOPK):
        m = jnp.max(work, axis=0, keepdims=True)
        sel = work == m
        if exact:
            sel = ids == jnp.min(jnp.where(sel, ids, 1e9), axis=0, keepdims=True)
        if want_rank:
            rank = jnp.where(sel, float(r), rank)
        work = jnp.where(sel, -jnp.inf, work)
        vals.append(m)
    return rank, vals


def _rows_to_array(rows):
    it = lax.broadcasted_iota(jnp.int32, (TOPK, LANES), 0)
    out = jnp.zeros((TOPK, LANES), F32)
    for r, row in enumerate(rows):
        out = jnp.where(it == r, row, out)
    return out


def _candidates(v1a, v2a, exact):
    it8 = lax.broadcasted_iota(jnp.int32, (8, LANES), 0)
    it16 = lax.broadcasted_iota(jnp.int32, (TOPK, LANES), 0)
    blocks = [v1a[0:1, :] + v2a]
    idb = [it16.astype(F32)]
    for k1 in range(1, 8):
        blocks.append(jnp.where(it8 < _CAND_CNT[k1], v1a[k1:k1 + 1, :] + v2a[0:8, :], -jnp.inf))
        idb.append((it8 + k1 * TOPK).astype(F32))
    blocks.append(v1a[8:16, :] + v2a[0:1, :])
    idb.append(((it8 + 8) * TOPK).astype(F32))
    cand = jnp.concatenate(blocks, axis=0)
    cid = jnp.concatenate(idb, axis=0)
    rc, top = _top16(cand, cid, exact)
    z = jnp.zeros((1, LANES), F32)
    for r in range(TOPK):
        z = z + jnp.exp(top[r] - top[0])
    picked = jnp.where(rc < float(TOPK), 1.0, 0.0)
    cnt_lo = jnp.zeros((8, LANES), F32)
    for k1 in range(8):
        lo = 0 if k1 == 0 else 8 + 8 * k1
        n = jnp.sum(picked[lo:lo + (TOPK if k1 == 0 else 8), :], axis=0, keepdims=True)
        cnt_lo = jnp.where(it8 == k1, n, cnt_lo)
    cnt = jnp.concatenate([cnt_lo, picked[72:80, :]], axis=0)
    return cnt, z


def _count_ne16(mask_f32):
    return jnp.where(jnp.sum(mask_f32, axis=0, keepdims=True) != float(TOPK), 1.0, 0.0)


def _peer_gates_exact(s1, s2):
    ids = lax.broadcasted_iota(jnp.int32, (N_KEYS, LANES), 0).astype(F32)
    r1, v1 = _top16(s1, ids, True)
    r2, v2 = _top16(s2, ids, True)
    cnt, z = _candidates(_rows_to_array(v1), _rows_to_array(v2), True)
    c1 = jnp.zeros((N_KEYS, LANES), F32)
    for k in range(TOPK):
        c1 = jnp.where(r1 == float(k), cnt[k:k + 1, :], c1)
    w1 = jnp.exp(s1 - v1[0]) * (0.5 / z)
    e2 = jnp.exp(s2 - v2[0])
    return r2, e2, c1, w1


def _peer_kernel(x_ref, xn_ref, mod_ref, modn_ref, g2_ref, gf_ref, wq_ref, sk_ref, u_ref, vt_ref, y_ref,
                 h2t_s, s_s, v_s, r2_s, e2_s, c1_s, w1_s, at_s, act_s, acc_s, redo_s, *, tt, ec):
    i, j = pl.program_id(0), pl.program_id(1)
    nlg = tt // LANES
    nb = ec // N_KEYS
    n_items = PEER_HEADS * nlg
    cur = i % 2
    nxt = 1 - cur
    want = float(TOPK)

    def project(xr, modr, par):
        x = xr[...]
        ms = jnp.mean(x * x, axis=-1, keepdims=True)
        xn = x * lax.rsqrt(ms + EPS) * g2_ref[...]
        h2 = xn * (1.0 + modr[0, 4:5, :]) + modr[0, 3:4, :]
        h2t_s[par] = h2.T.astype(BF16)
        per = ec // N_KEYS
        for part in range(2 * PEER_HEADS // per):
            act_s[:, 0:tt] = jnp.dot(wq_ref[part * ec:(part + 1) * ec, :], h2t_s[par],
                                     preferred_element_type=F32).astype(BF16)
            for k in range(per):
                s_s[part * per + k] = jnp.dot(sk_ref[part * per + k], act_s[k * N_KEYS:(k + 1) * N_KEYS, 0:tt],
                                              preferred_element_type=F32)

    def scores(idx):
        h, lg = idx // nlg, idx % nlg
        c0 = pl.multiple_of(lg * LANES, LANES)
        return h, lg, s_s[2 * h, :, pl.ds(c0, LANES)], s_s[2 * h + 1, :, pl.ds(c0, LANES)]

    def store_i2(par, h, lg, r2, e2):
        r2_s[par * PEER_HEADS + h, lg] = r2.astype(BF16)
        e2_s[par * PEER_HEADS + h, lg] = e2.astype(BF16)

    def store_i1(par, h, lg, c1, w1):
        c1_s[par * nlg + lg, pl.ds(h, N_KEYS, stride=PEER_HEADS), :] = c1
        w1_s[par * nlg + lg, pl.ds(h, N_KEYS, stride=PEER_HEADS), :] = w1

    def flag(row):
        return (jnp.max(row) > 0.0).astype(jnp.int32)

    def sides(idx, par, dep):
        h, lg, s1, s2 = scores(idx)
        _, v1 = _top16(s1 + dep, None, False, want_rank=False)
        r2, v2 = _top16(s2, None, False)
        store_i2(par, h, lg, r2, jnp.exp(s2 - v2[0]))
        v_s[idx, 0:TOPK, :] = _rows_to_array(v1)
        v_s[idx, TOPK:2 * TOPK, :] = _rows_to_array(v2)
        return (_count_ne16(jnp.where(s1 >= v1[TOPK - 1], 1.0, 0.0))
                + _count_ne16(jnp.where(r2 < want, 1.0, 0.0)))

    def cells(idx, par):
        h, lg, s1, _ = scores(idx)
        v1a = v_s[idx, 0:TOPK, :]
        cnt, z = _candidates(v1a, v_s[idx, TOPK:2 * TOPK, :], False)
        c1 = jnp.zeros((N_KEYS, LANES), F32)
        for k in range(TOPK):
            c1 = jnp.where(s1 == v1a[k:k + 1, :], cnt[k:k + 1, :], c1)
        store_i1(par, h, lg, c1, jnp.exp(s1 - v1a[0:1, :]) * (0.5 / z))
        redo_s[idx + 1] = redo_s[idx + 1] + flag(_count_ne16(cnt))

    @pl.when(j == 0)
    def _prologue():
        @pl.when(i == 0)
        def _first_tile():
            project(x_ref, mod_ref, 0)

            def side_trip(idx, prev_bad):
                redo_s[idx] = flag(prev_bad)
                return sides(idx, 0, jnp.zeros((1, LANES), F32))
            redo_s[n_items] = flag(lax.fori_loop(0, n_items, side_trip, jnp.zeros((1, LANES), F32)))

            def cell_trip(pair, carry):
                cells(2 * pair, 0)
                cells(2 * pair + 1, 0)
                return carry
            lax.fori_loop(0, n_items // 2, cell_trip, 0)

        def redo(idx, carry):
            @pl.when(redo_s[idx + 1] > 0)
            def _ties():
                h, lg, s1, s2 = scores(idx)
                r2, e2, c1, w1 = _peer_gates_exact(s1, s2)
                store_i2(cur, h, lg, r2, e2)
                store_i1(cur, h, lg, c1, w1)
            return carry
        lax.fori_loop(0, n_items, redo, 0)

        project(xn_ref, modn_ref, nxt)
        acc_s[...] = jnp.zeros_like(acc_s)

    a = jnp.dot(u_ref[...], h2t_s[cur], preferred_element_type=F32)
    at_s[:, 0:tt] = (a * (1.0 + lax.erf(a * (2.0 ** -0.5)))).astype(BF16)
    bad = sides(2 * j, nxt, jnp.zeros((1, LANES), F32))
    redo_s[2 * j + 1] = flag(bad)
    redo_s[2 * j + 2] = flag(sides(2 * j + 1, nxt, jnp.minimum(bad, 0.0)))

    n_sub = N_KEYS // GATE_ROWS
    n_bg = nb // GATE_BLOCKS

    def gate_trip(it, carry):
        l = it // (n_sub * n_bg)
        row0 = (it // n_bg % n_sub) * GATE_ROWS
        b0 = (it % n_bg) * GATE_BLOCKS
        c0 = pl.multiple_of(l * LANES, LANES)
        cb, wb = [], []
        for b in range(GATE_BLOCKS):
            i1 = pl.multiple_of((j * nb + b0 + b) * PEER_HEADS, PEER_HEADS)
            cb.append(c1_s[cur * nlg + l, pl.ds(i1, PEER_HEADS), :].astype(BF16))
            wb.append(w1_s[cur * nlg + l, pl.ds(i1, PEER_HEADS), :].astype(BF16))
        subs = range(GATE_ROWS // ROWS)
        g = [[jnp.zeros((ROWS, LANES), BF16) for _ in range(GATE_BLOCKS)] for _ in subs]
        zero = jnp.zeros((), BF16)
        for h in range(PEER_HEADS):
            for s in subs:
                rows = pl.ds(pl.multiple_of(row0 + s * ROWS, ROWS), ROWS)
                r2t = r2_s[cur * PEER_HEADS + h, l, rows, :]
                e2t = e2_s[cur * PEER_HEADS + h, l, rows, :]
                for b in range(GATE_BLOCKS):
                    g[s][b] = g[s][b] + e2t * jnp.where(r2t < cb[b][h:h + 1, :], wb[b][h:h + 1, :], zero)
        for s in subs:
            for b in range(GATE_BLOCKS):
                rows = pl.ds(pl.multiple_of((b0 + b) * N_KEYS + row0 + s * ROWS, ROWS), ROWS)
                act_s[rows, pl.ds(c0, LANES)] = g[s][b] * at_s[rows, pl.ds(c0, LANES)]
        return carry
    lax.fori_loop(0, nlg * n_sub * n_bg, gate_trip, 0)

    acc_s[:, 0:tt] += jnp.dot(vt_ref[0], act_s[:, 0:tt], preferred_element_type=F32)
    cells(2 * j, nxt)
    cells(2 * j + 1, nxt)

    @pl.when(j == pl.num_programs(1) - 1)
    def _epilogue():
        x2 = x_ref[...] + mod_ref[0, 5:6, :] * acc_s[:, 0:tt].T
        ms = jnp.mean(x2 * x2, axis=-1, keepdims=True)
        y_ref[...] = x2 * lax.rsqrt(ms + EPS) * gf_ref[...]


def _peer(x1, mod, mod_row, norm2_g, final_g, wq_t, sk, u_bf, vt_bf, tt, ec):
    t, d = x1.shape
    ne = u_bf.shape[0]
    nt, nc, nlg = t // tt, ne // ec, tt // LANES
    assert PEER_HEADS * nlg == 2 * nc
    scratch = [pltpu.VMEM((2, d, tt), BF16),
               pltpu.VMEM((2 * PEER_HEADS, N_KEYS, tt), F32),
               pltpu.VMEM((PEER_HEADS * nlg, 2 * TOPK, LANES), F32),
               pltpu.VMEM((2 * PEER_HEADS, nlg, N_KEYS, LANES), BF16),
               pltpu.VMEM((2 * PEER_HEADS, nlg, N_KEYS, LANES), BF16),
               pltpu.VMEM((2 * nlg, N_KEYS * PEER_HEADS, LANES), F32),
               pltpu.VMEM((2 * nlg, N_KEYS * PEER_HEADS, LANES), F32),
               pltpu.VMEM((ec, tt + LANES), BF16),
               pltpu.VMEM((ec, tt + LANES), BF16),
               pltpu.VMEM((d, tt + LANES), F32),
               pltpu.SMEM((PEER_HEADS * nlg + 1,), jnp.int32)]
    once = pl.Buffered(1)
    nxt = lambda i: jnp.minimum(i + 1, nt - 1)
    return pl.pallas_call(
        functools.partial(_peer_kernel, tt=tt, ec=ec),
        grid=(nt, nc),
        in_specs=[pl.BlockSpec((tt, d), lambda i, j: (i, 0), pipeline_mode=once),
                  pl.BlockSpec((tt, d), lambda i, j: (nxt(i), 0), pipeline_mode=once),
                  pl.BlockSpec((1, 6, d), lambda i, j: (mod_row(i), 0, 0)),
                  pl.BlockSpec((1, 6, d), lambda i, j: (mod_row(nxt(i)), 0, 0)),
                  pl.BlockSpec((1, d), lambda i, j: (0, 0)),
                  pl.BlockSpec((1, d), lambda i, j: (0, 0)),
                  pl.BlockSpec(wq_t.shape, lambda i, j: (0, 0), pipeline_mode=once),
                  pl.BlockSpec(sk.shape, lambda i, j: (0, 0, 0), pipeline_mode=once),
                  pl.BlockSpec((ec, d), lambda i, j: (j, 0)),
                  pl.BlockSpec((1, d, ec), lambda i, j: (j, 0, 0))],
        out_specs=pl.BlockSpec((tt, d), lambda i, j: (i, 0)),
        out_shape=jax.ShapeDtypeStruct((t, d), F32),
        scratch_shapes=scratch,
        compiler_params=_cparams(("arbitrary", "arbitrary")),
        name="peer_dense",
    )(x1, x1, mod, mod, norm2_g.reshape(1, d), final_g.reshape(1, d), wq_t, sk, u_bf, vt_bf)


def kernel(x_prompt, x_sample, cache_k, cache_v, c, c_ctx, w_mod, b_mod, norm1_g, w_in, lam_q1, lam_k1,
           lam_q2, lam_k2, subln_g, w_fourier, w_out, norm2_g, w_query, sub_keys, expert_u, expert_v,
           final_g):
    b, s, d = x_prompt.shape
    bd, sd, _ = x_sample.shape
    p = cache_k.shape[2]
    depth = w_mod.shape[0]
    assert depth == 1
    tm = 512
    tt, ec = 512, 1024

    rows = 16
    cond = jnp.zeros((rows, d), F32).at[0].set(c_ctx).at[1:1 + bd].set(c)
    tables = _rope_tables(sd)

    xp = x_prompt.reshape(b * s, d)
    xs = x_sample.reshape(bd * sd, d)
    l = 0
    lam_init = 0.8 - 0.6 * math.exp(-0.3 * l)
    mod = _modulation(cond, w_mod[l], b_mod[l]).reshape(rows, 6, d)
    w_in_bf = w_in[l].astype(BF16)
    w_out_bf = w_out[l].astype(BF16)
    w_f_bf = w_fourier[l].astype(BF16)
    wq_t = w_query[l].T.astype(BF16)
    sk = sub_keys[l].reshape(2 * PEER_HEADS, N_KEYS, -1).astype(BF16)
    u_bf = expert_u[l].astype(BF16)
    vt_bf = expert_v[l].reshape(-1, ec, d).transpose(0, 2, 1).astype(BF16)
    lams = (lam_q1[l], lam_k1[l], lam_q2[l], lam_k2[l])

    ctx_row = lambda i: 0
    lat_row = lambda i: 1 + i // (sd // tm)

    q, k, v, f = _inproj(xp, mod, norm1_g[l], w_in_bf, ctx_row, None, F32, tm)
    k3, v3 = k.reshape(b, s, ATT_W), v.reshape(b, s, ATT_W)
    att = _attention(q.reshape(b, s, ATT_W), (k3, v3), lams, subln_g[l], lam_init, s)
    four = _fourier(f.reshape(b, s, -1), w_f_bf)
    x1 = _outproj(xp, att.reshape(b * s, ATT_W), four.reshape(b * s, -1), w_out_bf, mod, ctx_row, tm)
    y_prompt = _peer(x1, mod, ctx_row, norm2_g[l], final_g, wq_t, sk, u_bf, vt_bf, tt, ec)
    state_k = k.reshape(b, 1, s, N_HEADS, 2 * HEAD_DIM)
    state_v = v.reshape(b, 1, s, N_HEADS, 2 * HEAD_DIM)

    q, k, v, f = _inproj(xs, mod, norm1_g[l], w_in_bf, lat_row, tables, BF16, tm)
    ck = cache_k[:, l].reshape(bd, p, ATT_W)
    cv = cache_v[:, l].reshape(bd, p, ATT_W)
    att = _attention(q.reshape(bd, sd, ATT_W),
                     (ck, cv, k.reshape(bd, sd, ATT_W), v.reshape(bd, sd, ATT_W)),
                     lams, subln_g[l], lam_init, 512)
    four = _fourier(f.reshape(bd, sd, -1), w_f_bf)
    x1 = _outproj(xs, att.reshape(bd * sd, ATT_W), four.reshape(bd * sd, -1), w_out_bf, mod, lat_row, tm)
    y_sample = _peer(x1, mod, lat_row, norm2_g[l], final_g, wq_t, sk, u_bf, vt_bf, tt, ec)

    return (y_prompt.reshape(b, s, d), y_sample.reshape(bd, sd, d), state_k, state_v)
```

```python
import functools
import math

import jax
import jax.numpy as jnp
import numpy as np
from jax import lax
from jax.experimental import pallas as pl
from jax.experimental.pallas import tpu as pltpu

F32 = jnp.float32
BF16 = jnp.bfloat16

EPS = 1e-6
LANES = 128
GRID_W = 64
N_HEADS = 4
HEAD_DIM = 64
ATT_W = N_HEADS * 2 * HEAD_DIM
N_GROUPS = 4
GROUP_DIM = 128
ROPE_BASE = 10000.0
PEER_HEADS = 8
N_KEYS = 128
TOPK = 16
ROWS = 16
GATE_BLOCKS = 8
GATE_ROWS = 128
VMEM_LIMIT = 56 * 1024 * 1024

_CAND_CNT = [TOPK // (k + 1) for k in range(TOPK)]


def _cparams(sem):
    return pltpu.CompilerParams(dimension_semantics=sem, vmem_limit_bytes=VMEM_LIMIT)


def _mod_kernel(c_ref, w_ref, b_ref, o_ref):
    c = c_ref[...]
    a = c / (1.0 + jnp.exp(-c))
    o_ref[...] = jnp.dot(a.astype(BF16), w_ref[...].astype(BF16), preferred_element_type=F32) + b_ref[...]


def _modulation(cond, w_mod, b_mod):
    rows, d = cond.shape
    n = w_mod.shape[1]
    tn = 1536
    return pl.pallas_call(
        _mod_kernel,
        grid=(n // tn,),
        in_specs=[pl.BlockSpec((rows, d), lambda j: (0, 0)),
                  pl.BlockSpec((d, tn), lambda j: (0, j)),
                  pl.BlockSpec((1, tn), lambda j: (0, j))],
        out_specs=pl.BlockSpec((rows, tn), lambda j: (0, j)),
        out_shape=jax.ShapeDtypeStruct((rows, n), F32),
        compiler_params=_cparams(("parallel",)),
        name="modulation",
    )(cond, w_mod, b_mod.reshape(1, n))


def _inproj_kernel(*refs, rope):
    if rope:
        x_ref, mod_ref, g_ref, w_ref, cos_ref, sa_ref, sb_ref, q_ref, k_ref, v_ref, f_ref = refs
    else:
        x_ref, mod_ref, g_ref, w_ref, q_ref, k_ref, v_ref, f_ref = refs
    x = x_ref[...]
    ms = jnp.mean(x * x, axis=-1, keepdims=True)
    xn = x * lax.rsqrt(ms + EPS) * g_ref[...]
    h = xn * (1.0 + mod_ref[0, 1:2, :]) + mod_ref[0, 0:1, :]
    z = jnp.dot(h.astype(BF16), w_ref[...], preferred_element_type=F32)
    scale = HEAD_DIM ** -0.5
    if rope:
        cos, sa, sb = cos_ref[...], sa_ref[...], sb_ref[...]
        for ch in range(ATT_W // LANES):
            lo, hi = ch * LANES, (ch + 1) * LANES
            for base, ref, sc in ((0, q_ref, scale), (ATT_W, k_ref, 1.0)):
                t = z[:, base + lo:base + hi]
                r = t * cos + pltpu.roll(t, LANES - 16, 1) * sa + pltpu.roll(t, 16, 1) * sb
                ref[:, lo:hi] = (r * sc).astype(ref.dtype)
    else:
        q_ref[...] = (z[:, :ATT_W] * scale).astype(q_ref.dtype)
        k_ref[...] = z[:, ATT_W:2 * ATT_W].astype(k_ref.dtype)
    v_ref[...] = z[:, 2 * ATT_W:3 * ATT_W].astype(v_ref.dtype)
    f_ref[...] = z[:, 3 * ATT_W:].astype(f_ref.dtype)


def _inproj(x2d, mod, norm_g, w_in_bf, mod_row, tables, kv_dtype, tm):
    t, d = x2d.shape
    n = w_in_bf.shape[1]
    fw = n - 3 * ATT_W
    rope = tables is not None
    in_specs = [pl.BlockSpec((tm, d), lambda i: (i, 0)),
                pl.BlockSpec((1, 6, d), lambda i: (mod_row(i), 0, 0)),
                pl.BlockSpec((1, d), lambda i: (0, 0)),
                pl.BlockSpec((d, n), lambda i: (0, 0))]
    args = [x2d, mod, norm_g.reshape(1, d), w_in_bf]
    if rope:
        seq = tables[0].shape[0]
        per = seq // tm
        for tb in tables:
            in_specs.append(pl.BlockSpec((tm, LANES), lambda i: (i % per, 0)))
            args.append(tb)
    return pl.pallas_call(
        functools.partial(_inproj_kernel, rope=rope),
        grid=(t // tm,),
        in_specs=in_specs,
        out_specs=[pl.BlockSpec((tm, ATT_W), lambda i: (i, 0)),
                   pl.BlockSpec((tm, ATT_W), lambda i: (i, 0)),
                   pl.BlockSpec((tm, ATT_W), lambda i: (i, 0)),
                   pl.BlockSpec((tm, fw), lambda i: (i, 0))],
        out_shape=[jax.ShapeDtypeStruct((t, ATT_W), BF16),
                   jax.ShapeDtypeStruct((t, ATT_W), kv_dtype),
                   jax.ShapeDtypeStruct((t, ATT_W), kv_dtype),
                   jax.ShapeDtypeStruct((t, fw), BF16)],
        compiler_params=_cparams(("parallel",)),
        name="inproj_rope" if rope else "inproj",
    )(*args)


def _rope_tables(seq):
    rows = seq // GRID_W
    row = np.repeat(np.arange(rows), GRID_W).astype(np.float32)
    col = np.tile(np.arange(GRID_W), rows).astype(np.float32)
    nf = HEAD_DIM // 4
    inv = jnp.asarray(ROPE_BASE, F32) ** (-jnp.arange(nf, dtype=F32) / nf)
    ar = jnp.asarray(row)[:, None] * inv
    ac = jnp.asarray(col)[:, None] * inv
    ang = jnp.concatenate([ar, ar, ac, ac], axis=-1)
    cos = jnp.tile(jnp.cos(ang), (1, LANES // HEAD_DIM))
    sin = jnp.tile(jnp.sin(ang), (1, LANES // HEAD_DIM))
    first = (np.arange(LANES) % 32) < 16
    sa = jnp.where(first[None, :], -sin, 0.0)
    sb = jnp.where(first[None, :], 0.0, sin)
    return cos, sa, sb


def _attn_kernel(*refs, n_pieces, lam_init):
    lq1, lk1, lq2, lk2, sg_ref, q_ref = refs[:6]
    kv = refs[6:6 + 2 * n_pieces]
    o_ref = refs[6 + 2 * n_pieces]
    l1 = jnp.sum(lq1[...] * lk1[...], axis=-1, keepdims=True)
    l2 = jnp.sum(lq2[...] * lk2[...], axis=-1, keepdims=True)
    lam = jnp.exp(l1) - jnp.exp(l2) + lam_init
    lane = lax.broadcasted_iota(jnp.int32, (1, LANES), 1)
    nt = (((1,), (1,)), ((), ()))
    for h in range(N_HEADS):
        hs = slice(h * LANES, (h + 1) * LANES)
        qh = q_ref[0, :, hs]
        s0, s1, vs = [], [], []
        for p in range(n_pieces):
            kp = kv[2 * p][0, :, hs].astype(BF16)
            vs.append(kv[2 * p + 1][0, :, hs].astype(BF16))
            k0 = jnp.where(lane < HEAD_DIM, kp, jnp.zeros_like(kp))
            k1 = jnp.where(lane >= HEAD_DIM, kp, jnp.zeros_like(kp))
            s0.append(lax.dot_general(qh, k0, nt, preferred_element_type=F32))
            s1.append(lax.dot_general(qh, k1, nt, preferred_element_type=F32))
        outs = []
        for ss in (s0, s1):
            m = ss[0].max(axis=-1, keepdims=True)
            for s in ss[1:]:
                m = jnp.maximum(m, s.max(axis=-1, keepdims=True))
            den, num = None, None
            for s, vp in zip(ss, vs):
                e = jnp.exp(s - m)
                d = e.sum(axis=-1, keepdims=True)
                t = jnp.dot(e.astype(BF16), vp, preferred_element_type=F32)
                den = d if den is None else den + d
                num = t if num is None else num + t
            outs.append(num * (1.0 / den))
        o = outs[0] - lam * outs[1]
        ms = jnp.mean(o * o, axis=-1, keepdims=True)
        a = o * lax.rsqrt(ms + EPS) * sg_ref[...] * (1.0 - lam_init)
        o_ref[0, :, hs] = a.astype(o_ref.dtype)


def _attention(q, kvs, lams, subln_g, lam_init, tq):
    b, sq, w = q.shape
    n_pieces = len(kvs) // 2
    lam_specs = [pl.BlockSpec((1, HEAD_DIM), lambda i, j: (0, 0)) for _ in range(4)]
    in_specs = lam_specs + [pl.BlockSpec((1, LANES), lambda i, j: (0, 0)),
                            pl.BlockSpec((1, tq, w), lambda i, j: (i, j, 0))]
    for a in kvs:
        in_specs.append(pl.BlockSpec((1, a.shape[1], w), lambda i, j: (i, 0, 0)))
    return pl.pallas_call(
        functools.partial(_attn_kernel, n_pieces=n_pieces, lam_init=lam_init),
        grid=(b, sq // tq),
        in_specs=in_specs,
        out_specs=pl.BlockSpec((1, tq, w), lambda i, j: (i, j, 0)),
        out_shape=jax.ShapeDtypeStruct((b, sq, w), BF16),
        compiler_params=_cparams(("parallel", "parallel")),
        name="diff_attention_%d" % n_pieces,
    )(*[l.reshape(1, HEAD_DIM) for l in lams], subln_g.reshape(1, LANES), q, *kvs)


def _fourier_kernel(f_ref, wcs_ref, m2_ref, wf_ref, o_ref):
    gslices = [slice(g * GROUP_DIM, (g + 1) * GROUP_DIM) for g in range(N_GROUPS)]
    ys = [jnp.dot(f_ref[0, :, gs], wcs_ref[...], preferred_element_type=F32) for gs in gslices]
    yc = jnp.concatenate([jnp.concatenate([y[:, :GROUP_DIM] for y in ys], axis=1),
                          jnp.concatenate([y[:, GROUP_DIM:] for y in ys], axis=1)], axis=0).astype(BF16)
    spec = jnp.dot(m2_ref[...], yc, preferred_element_type=F32).astype(BF16)
    for g, gs in enumerate(gslices):
        o_ref[0, :, gs] = jnp.dot(spec[:, gs], wf_ref[g], preferred_element_type=F32).astype(o_ref.dtype)


def _dft_tables(seq):
    kc = np.arange(GROUP_DIM)
    ang_c = 2.0 * np.pi * ((kc[:, None] * kc[None, :]) % GROUP_DIM) / GROUP_DIM
    wcs = np.concatenate([np.cos(ang_c), np.sin(ang_c)], axis=1) / math.sqrt(GROUP_DIM)
    ks = np.arange(seq)
    ang_s = 2.0 * np.pi * ((ks[:, None] * ks[None, :]) % seq) / seq
    m2 = np.concatenate([np.cos(ang_s), -np.sin(ang_s)], axis=1) / math.sqrt(seq)
    return jnp.asarray(wcs, F32).astype(BF16), jnp.asarray(m2, F32).astype(BF16)


def _fourier(f, w_f_bf):
    b, s, w = f.shape
    wcs, m2 = _dft_tables(s)
    return pl.pallas_call(
        _fourier_kernel,
        grid=(b,),
        in_specs=[pl.BlockSpec((1, s, w), lambda i: (i, 0, 0)),
                  pl.BlockSpec(wcs.shape, lambda i: (0, 0)),
                  pl.BlockSpec(m2.shape, lambda i: (0, 0)),
                  pl.BlockSpec(w_f_bf.shape, lambda i: (0, 0, 0))],
        out_specs=pl.BlockSpec((1, s, w), lambda i: (i, 0, 0)),
        out_shape=jax.ShapeDtypeStruct((b, s, w), BF16),
        compiler_params=_cparams(("parallel",)),
        name="fourier_mix",
    )(f, wcs, m2, w_f_bf)


def _outproj_kernel(x_ref, a_ref, f_ref, wa_ref, wf_ref, mod_ref, o_ref):
    m = jnp.dot(a_ref[...], wa_ref[...], preferred_element_type=F32)
    m = m + jnp.dot(f_ref[...], wf_ref[...], preferred_element_type=F32)
    o_ref[...] = x_ref[...] + mod_ref[0, 2:3, :] * m


def _outproj(x2d, a2d, f2d, w_out_bf, mod, mod_row, tm):
    t, d = x2d.shape
    wa, wf = w_out_bf[:ATT_W], w_out_bf[ATT_W:]
    return pl.pallas_call(
        _outproj_kernel,
        grid=(t // tm,),
        in_specs=[pl.BlockSpec((tm, d), lambda i: (i, 0)),
                  pl.BlockSpec((tm, ATT_W), lambda i: (i, 0)),
                  pl.BlockSpec((tm, wf.shape[0]), lambda i: (i, 0)),
                  pl.BlockSpec(wa.shape, lambda i: (0, 0)),
                  pl.BlockSpec(wf.shape, lambda i: (0, 0)),
                  pl.BlockSpec((1, 6, d), lambda i: (mod_row(i), 0, 0))],
        out_specs=pl.BlockSpec((tm, d), lambda i: (i, 0)),
        out_shape=jax.ShapeDtypeStruct((t, d), F32),
        compiler_params=_cparams(("parallel",)),
        name="outproj",
    )(x2d, a2d, f2d, wa, wf, mod)


def _top16(s, ids, exact, want_rank=True):
    work = s
    rank = jnp.full(s.shape, float(TOPK), F32) if want_rank else None
    vals = []
    for r in range(TOPK):
        m = jnp.max(work, axis=0, keepdims=True)
        sel = work == m
        if exact:
            sel = ids == jnp.min(jnp.where(sel, ids, 1e9), axis=0, keepdims=True)
        if want_rank:
            rank = jnp.where(sel, float(r), rank)
        work = jnp.where(sel, -jnp.inf, work)
        vals.append(m)
    return rank, vals


def _rows_to_array(rows):
    it = lax.broadcasted_iota(jnp.int32, (TOPK, LANES), 0)
    out = jnp.zeros((TOPK, LANES), F32)
    for r, row in enumerate(rows):
        out = jnp.where(it == r, row, out)
    return out


def _candidates(v1a, v2a, exact):
    it8 = lax.broadcasted_iota(jnp.int32, (8, LANES), 0)
    it16 = lax.broadcasted_iota(jnp.int32, (TOPK, LANES), 0)
    blocks = [v1a[0:1, :] + v2a]
    idb = [it16.astype(F32)]
    for k1 in range(1, 8):
        blocks.append(jnp.where(it8 < _CAND_CNT[k1], v1a[k1:k1 + 1, :] + v2a[0:8, :], -jnp.inf))
        idb.append((it8 + k1 * TOPK).astype(F32))
    blocks.append(v1a[8:16, :] + v2a[0:1, :])
    idb.append(((it8 + 8) * TOPK).astype(F32))
    cand = jnp.concatenate(blocks, axis=0)
    cid = jnp.concatenate(idb, axis=0)
    rc, top = _top16(cand, cid, exact)
    z = jnp.zeros((1, LANES), F32)
    for r in range(TOPK):
        z = z + jnp.exp(top[r] - top[0])
    picked = jnp.where(rc < float(TOPK), 1.0, 0.0)
    cnt_lo = jnp.zeros((8, LANES), F32)
    for k1 in range(8):
        lo = 0 if k1 == 0 else 8 + 8 * k1
        n = jnp.sum(picked[lo:lo + (TOPK if k1 == 0 else 8), :], axis=0, keepdims=True)
        cnt_lo = jnp.where(it8 == k1, n, cnt_lo)
    cnt = jnp.concatenate([cnt_lo, picked[72:80, :]], axis=0)
    return cnt, z


def _count_ne16(mask_f32):
    return jnp.where(jnp.sum(mask_f32, axis=0, keepdims=True) != float(TOPK), 1.0, 0.0)


def _peer_gates_exact(s1, s2):
    ids = lax.broadcasted_iota(jnp.int32, (N_KEYS, LANES), 0).astype(F32)
    r1, v1 = _top16(s1, ids, True)
    r2, v2 = _top16(s2, ids, True)
    cnt, z = _candidates(_rows_to_array(v1), _rows_to_array(v2), True)
    c1 = jnp.zeros((N_KEYS, LANES), F32)
    for k in range(TOPK):
        c1 = jnp.where(r1 == float(k), cnt[k:k + 1, :], c1)
    w1 = jnp.exp(s1 - v1[0]) * (0.5 / z)
    e2 = jnp.exp(s2 - v2[0])
    return r2, e2, c1, w1


def _peer_kernel(x_ref, xn_ref, mod_ref, modn_ref, g2_ref, gf_ref, wq_ref, sk_ref, u_ref, vt_ref, y_ref,
                 h2t_s, s_s, v_s, r2_s, e2_s, c1_s, w1_s, at_s, act_s, acc_s, redo_s, *, tt, ec):
    i, j = pl.program_id(0), pl.program_id(1)
    nlg = tt // LANES
    nb = ec // N_KEYS
    n_items = PEER_HEADS * nlg
    cur = i % 2
    nxt = 1 - cur
    want = float(TOPK)

    def project(xr, modr, par):
        x = xr[...]
        ms = jnp.mean(x * x, axis=-1, keepdims=True)
        xn = x * lax.rsqrt(ms + EPS) * g2_ref[...]
        h2 = xn * (1.0 + modr[0, 4:5, :]) + modr[0, 3:4, :]
        h2t_s[par] = h2.T.astype(BF16)
        per = ec // N_KEYS
        for part in range(2 * PEER_HEADS // per):
            act_s[:, 0:tt] = jnp.dot(wq_ref[part * ec:(part + 1) * ec, :], h2t_s[par],
                                     preferred_element_type=F32).astype(BF16)
            for k in range(per):
                s_s[part * per + k] = jnp.dot(sk_ref[part * per + k], act_s[k * N_KEYS:(k + 1) * N_KEYS, 0:tt],
                                              preferred_element_type=F32)

    def scores(idx):
        h, lg = idx // nlg, idx % nlg
        c0 = pl.multiple_of(lg * LANES, LANES)
        return h, lg, s_s[2 * h, :, pl.ds(c0, LANES)], s_s[2 * h + 1, :, pl.ds(c0, LANES)]

    def store_i2(par, h, lg, r2, e2):
        r2_s[par * PEER_HEADS + h, lg] = r2.astype(BF16)
        e2_s[par * PEER_HEADS + h, lg] = e2.astype(BF16)

    def store_i1(par, h, lg, c1, w1):
        c1_s[par * nlg + lg, pl.ds(h, N_KEYS, stride=PEER_HEADS), :] = c1
        w1_s[par * nlg + lg, pl.ds(h, N_KEYS, stride=PEER_HEADS), :] = w1

    def flag(row):
        return (jnp.max(row) > 0.0).astype(jnp.int32)

    def sides(idx, par, dep):
        h, lg, s1, s2 = scores(idx)
        _, v1 = _top16(s1 + dep, None, False, want_rank=False)
        r2, v2 = _top16(s2, None, False)
        store_i2(par, h, lg, r2, jnp.exp(s2 - v2[0]))
        v_s[idx, 0:TOPK, :] = _rows_to_array(v1)
        v_s[idx, TOPK:2 * TOPK, :] = _rows_to_array(v2)
        return (_count_ne16(jnp.where(s1 >= v1[TOPK - 1], 1.0, 0.0))
                + _count_ne16(jnp.where(r2 < want, 1.0, 0.0)))

    def cells(idx, par):
        h, lg, s1, _ = scores(idx)
        v1a = v_s[idx, 0:TOPK, :]
        cnt, z = _candidates(v1a, v_s[idx, TOPK:2 * TOPK, :], False)
        c1 = jnp.zeros((N_KEYS, LANES), F32)
        for k in range(TOPK):
            c1 = jnp.where(s1 == v1a[k:k + 1, :], cnt[k:k + 1, :], c1)
        store_i1(par, h, lg, c1, jnp.exp(s1 - v1a[0:1, :]) * (0.5 / z))
        redo_s[idx + 1] = redo_s[idx + 1] + flag(_count_ne16(cnt))

    @pl.when(j == 0)
    def _prologue():
        @pl.when(i == 0)
        def _first_tile():
            project(x_ref, mod_ref, 0)

            def side_trip(idx, prev_bad):
                redo_s[idx] = flag(prev_bad)
                return sides(idx, 0, jnp.zeros((1, LANES), F32))
            redo_s[n_items] = flag(lax.fori_loop(0, n_items, side_trip, jnp.zeros((1, LANES), F32)))

            def cell_trip(pair, carry):
                cells(2 * pair, 0)
                cells(2 * pair + 1, 0)
                return carry
            lax.fori_loop(0, n_items // 2, cell_trip, 0)

        def redo(idx, carry):
            @pl.when(redo_s[idx + 1] > 0)
            def _ties():
                h, lg, s1, s2 = scores(idx)
                r2, e2, c1, w1 = _peer_gates_exact(s1, s2)
                store_i2(cur, h, lg, r2, e2)
                store_i1(cur, h, lg, c1, w1)
            return carry
        lax.fori_loop(0, n_items, redo, 0)

        project(xn_ref, modn_ref, nxt)
        acc_s[...] = jnp.zeros_like(acc_s)

    half = ec // 2

    def pre_act(r0):
        a = jnp.dot(u_ref[pl.ds(r0, half), :], h2t_s[cur], preferred_element_type=F32)
        at_s[pl.ds(r0, half), 0:tt] = (a * (1.0 + lax.erf(a * (2.0 ** -0.5)))).astype(BF16)

    bad = sides(2 * j, nxt, jnp.zeros((1, LANES), F32))
    f0 = flag(bad)
    redo_s[2 * j + 1] = f0
    pre_act(0)
    redo_s[2 * j + 2] = flag(sides(2 * j + 1, nxt, jnp.minimum(bad, 0.0)))
    pre_act(pl.multiple_of(half + jnp.minimum(f0, 0) * ROWS, ROWS))

    n_sub = N_KEYS // GATE_ROWS
    n_bg = nb // GATE_BLOCKS

    def gate_trip(it, carry):
        l = it // (n_sub * n_bg)
        row0 = (it // n_bg % n_sub) * GATE_ROWS
        b0 = (it % n_bg) * GATE_BLOCKS
        c0 = pl.multiple_of(l * LANES, LANES)
        cb, wb = [], []
        for b in range(GATE_BLOCKS):
            i1 = pl.multiple_of((j * nb + b0 + b) * PEER_HEADS, PEER_HEADS)
            cb.append(c1_s[cur * nlg + l, pl.ds(i1, PEER_HEADS), :].astype(BF16))
            wb.append(w1_s[cur * nlg + l, pl.ds(i1, PEER_HEADS), :].astype(BF16))
        subs = range(GATE_ROWS // ROWS)
        g = [[jnp.zeros((ROWS, LANES), BF16) for _ in range(GATE_BLOCKS)] for _ in subs]
        zero = jnp.zeros((), BF16)
        for h in range(PEER_HEADS):
            for s in subs:
                rows = pl.ds(pl.multiple_of(row0 + s * ROWS, ROWS), ROWS)
                r2t = r2_s[cur * PEER_HEADS + h, l, rows, :]
                e2t = e2_s[cur * PEER_HEADS + h, l, rows, :]
                for b in range(GATE_BLOCKS):
                    g[s][b] = g[s][b] + e2t * jnp.where(r2t < cb[b][h:h + 1, :], wb[b][h:h + 1, :], zero)
        for s in subs:
            for b in range(GATE_BLOCKS):
                rows = pl.ds(pl.multiple_of((b0 + b) * N_KEYS + row0 + s * ROWS, ROWS), ROWS)
                act_s[rows, pl.ds(c0, LANES)] = g[s][b] * at_s[rows, pl.ds(c0, LANES)]
        return carry
    lax.fori_loop(0, nlg * n_sub * n_bg, gate_trip, 0)

    hd = acc_s.shape[0] // 2

    def accumulate(r0):
        acc_s[pl.ds(r0, hd), 0:tt] += jnp.dot(vt_ref[0, pl.ds(r0, hd), :], act_s[:, 0:tt],
                                              preferred_element_type=F32)

    cells(2 * j, nxt)
    accumulate(0)
    f1 = redo_s[2 * j + 1]
    cells(2 * j + 1, nxt)
    accumulate(pl.multiple_of(hd + jnp.minimum(f1, 0) * 8, 8))

    @pl.when(j == pl.num_programs(1) - 1)
    def _epilogue():
        x2 = x_ref[...] + mod_ref[0, 5:6, :] * acc_s[:, 0:tt].T
        ms = jnp.mean(x2 * x2, axis=-1, keepdims=True)
        y_ref[...] = x2 * lax.rsqrt(ms + EPS) * gf_ref[...]


def _peer(x1, mod, mod_row, norm2_g, final_g, wq_t, sk, u_bf, vt_bf, tt, ec):
    t, d = x1.shape
    ne = u_bf.shape[0]
    nt, nc, nlg = t // tt, ne // ec, tt // LANES
    assert PEER_HEADS * nlg == 2 * nc
    scratch = [pltpu.VMEM((2, d, tt), BF16),
               pltpu.VMEM((2 * PEER_HEADS, N_KEYS, tt), F32),
               pltpu.VMEM((PEER_HEADS * nlg, 2 * TOPK, LANES), F32),
               pltpu.VMEM((2 * PEER_HEADS, nlg, N_KEYS, LANES), BF16),
               pltpu.VMEM((2 * PEER_HEADS, nlg, N_KEYS, LANES), BF16),
               pltpu.VMEM((2 * nlg, N_KEYS * PEER_HEADS, LANES), F32),
               pltpu.VMEM((2 * nlg, N_KEYS * PEER_HEADS, LANES), F32),
               pltpu.VMEM((ec, tt + LANES), BF16),
               pltpu.VMEM((ec, tt + LANES), BF16),
               pltpu.VMEM((d, tt + LANES), F32),
               pltpu.SMEM((PEER_HEADS * nlg + 1,), jnp.int32)]
    once = pl.Buffered(1)
    nxt = lambda i: jnp.minimum(i + 1, nt - 1)
    return pl.pallas_call(
        functools.partial(_peer_kernel, tt=tt, ec=ec),
        grid=(nt, nc),
        in_specs=[pl.BlockSpec((tt, d), lambda i, j: (i, 0), pipeline_mode=once),
                  pl.BlockSpec((tt, d), lambda i, j: (nxt(i), 0), pipeline_mode=once),
                  pl.BlockSpec((1, 6, d), lambda i, j: (mod_row(i), 0, 0)),
                  pl.BlockSpec((1, 6, d), lambda i, j: (mod_row(nxt(i)), 0, 0)),
                  pl.BlockSpec((1, d), lambda i, j: (0, 0)),
                  pl.BlockSpec((1, d), lambda i, j: (0, 0)),
                  pl.BlockSpec(wq_t.shape, lambda i, j: (0, 0), pipeline_mode=once),
                  pl.BlockSpec(sk.shape, lambda i, j: (0, 0, 0), pipeline_mode=once),
                  pl.BlockSpec((ec, d), lambda i, j: (j, 0)),
                  pl.BlockSpec((1, d, ec), lambda i, j: (j, 0, 0))],
        out_specs=pl.BlockSpec((tt, d), lambda i, j: (i, 0)),
        out_shape=jax.ShapeDtypeStruct((t, d), F32),
        scratch_shapes=scratch,
        compiler_params=_cparams(("arbitrary", "arbitrary")),
        name="peer_dense",
    )(x1, x1, mod, mod, norm2_g.reshape(1, d), final_g.reshape(1, d), wq_t, sk, u_bf, vt_bf)


def kernel(x_prompt, x_sample, cache_k, cache_v, c, c_ctx, w_mod, b_mod, norm1_g, w_in, lam_q1, lam_k1,
           lam_q2, lam_k2, subln_g, w_fourier, w_out, norm2_g, w_query, sub_keys, expert_u, expert_v,
           final_g):
    b, s, d = x_prompt.shape
    bd, sd, _ = x_sample.shape
    p = cache_k.shape[2]
    depth = w_mod.shape[0]
    assert depth == 1
    tm = 1024
    tt, ec = 512, 1024

    rows = 16
    cond = jnp.zeros((rows, d), F32).at[0].set(c_ctx).at[1:1 + bd].set(c)
    tables = _rope_tables(sd)

    xp = x_prompt.reshape(b * s, d)
    xs = x_sample.reshape(bd * sd, d)
    l = 0
    lam_init = 0.8 - 0.6 * math.exp(-0.3 * l)
    mod = _modulation(cond, w_mod[l], b_mod[l]).reshape(rows, 6, d)
    w_in_bf = w_in[l].astype(BF16)
    w_out_bf = w_out[l].astype(BF16)
    w_f_bf = w_fourier[l].astype(BF16)
    wq_t = w_query[l].T.astype(BF16)
    sk = sub_keys[l].reshape(2 * PEER_HEADS, N_KEYS, -1).astype(BF16)
    u_bf = expert_u[l].astype(BF16)
    vt_bf = expert_v[l].reshape(-1, ec, d).transpose(0, 2, 1).astype(BF16)
    lams = (lam_q1[l], lam_k1[l], lam_q2[l], lam_k2[l])

    ctx_row = lambda i: 0
    lat_row = lambda i: 1 + i // (sd // tm)
    lat_row_peer = lambda i: 1 + i // (sd // tt)

    q, k, v, f = _inproj(xp, mod, norm1_g[l], w_in_bf, ctx_row, None, F32, tm)
    k3, v3 = k.reshape(b, s, ATT_W), v.reshape(b, s, ATT_W)
    att = _attention(q.reshape(b, s, ATT_W), (k3, v3), lams, subln_g[l], lam_init, s)
    four = _fourier(f.reshape(b, s, -1), w_f_bf)
    x1 = _outproj(xp, att.reshape(b * s, ATT_W), four.reshape(b * s, -1), w_out_bf, mod, ctx_row, tm)
    y_prompt = _peer(x1, mod, ctx_row, norm2_g[l], final_g, wq_t, sk, u_bf, vt_bf, tt, ec)
    state_k = k.reshape(b, 1, s, N_HEADS, 2 * HEAD_DIM)
    state_v = v.reshape(b, 1, s, N_HEADS, 2 * HEAD_DIM)

    q, k, v, f = _inproj(xs, mod, norm1_g[l], w_in_bf, lat_row, tables, BF16, tm)
    ck = cache_k[:, l].reshape(bd, p, ATT_W)
    cv = cache_v[:, l].reshape(bd, p, ATT_W)
    att = _attention(q.reshape(bd, sd, ATT_W),
                     (ck, cv, k.reshape(bd, sd, ATT_W), v.reshape(bd, sd, ATT_W)),
                     lams, subln_g[l], lam_init, 512)
    four = _fourier(f.reshape(bd, sd, -1), w_f_bf)
    x1 = _outproj(xs, att.reshape(bd * sd, ATT_W), four.reshape(bd * sd, -1), w_out_bf, mod, lat_row, tm)
    y_sample = _peer(x1, mod, lat_row_peer, norm2_g[l], final_g, wq_t, sk, u_bf, vt_bf, tt, ec)

    return (y_prompt.reshape(b, s, d), y_sample.reshape(bd, sd, d), state_k, state_v)
```
